```python
import math
import jax
import jax.numpy as jnp
from jax import lax
import numpy as np

D_MODEL = 4096
BATCH = 2
SEQ = 8192
DEPTH = 4

GRID_W = 64
CTX_LEN = 256
EPS = 1e-6
NEG_INF = -1e30
ROPE_BASE = 10000.0

N_BRANCHES = 4
BRANCH_WIDTH = D_MODEL // 4
FOURIER_GROUPS = 4
FOURIER_GROUP_DIM = BRANCH_WIDTH // FOURIER_GROUPS
CONV_K = 3
SWA_HEADS = 8
SWA_KV_HEADS = 2
SWA_HEAD_DIM = BRANCH_WIDTH // SWA_HEADS
SWA_WINDOW = 128
SWA_BLOCK = 128
DIFF_HEADS = 8
DIFF_HEAD_DIM = BRANCH_WIDTH // (2 * DIFF_HEADS)
DIFF_BLOCK = 128
GATE_RANK = 256
ADA_RANK = 256
N_MOD = 9
D_FF = 5120
IN_SPLITS = (BRANCH_WIDTH, BRANCH_WIDTH, BRANCH_WIDTH, BRANCH_WIDTH,
             SWA_HEADS * SWA_HEAD_DIM, SWA_KV_HEADS * SWA_HEAD_DIM, SWA_KV_HEADS * SWA_HEAD_DIM,
             2 * DIFF_HEADS * DIFF_HEAD_DIM, 2 * DIFF_HEADS * DIFF_HEAD_DIM, 2 * DIFF_HEADS * DIFF_HEAD_DIM,
             GATE_RANK)
IN_COLS = sum(IN_SPLITS)

kernel_name = 'hybrid_parallel_mixer_dit'


def rms_norm(x, g):
    xf = x.astype(jnp.float32)
    y = xf * lax.rsqrt(jnp.mean(xf * xf, axis=-1, keepdims=True) + EPS)
    return (y * g.astype(jnp.float32)).astype(x.dtype)


def modulate(h, shift, scale):
    return h * (1.0 + scale) + shift


def ada_modulation(cvec, down, up, b):
    m = (jax.nn.silu(cvec) @ down) @ up + b
    return jnp.split(m[..., None, :], N_MOD, axis=-1)


def ffn_half_step(t, shift, scale, gate, g, w13, w2):
    h = modulate(rms_norm(t, g), shift, scale)
    a, u = jnp.split(h @ w13, 2, axis=-1)
    return t + 0.5 * gate * ((jax.nn.silu(a) * u) @ w2)


def axial_rope_tables(row, col, head_dim):
    axis_dim = head_dim // 2
    inv_freq = ROPE_BASE ** (-jnp.arange(0, axis_dim, 2, dtype=jnp.float32) / axis_dim)
    ar = row.astype(jnp.float32)[:, None] * inv_freq
    ac = col.astype(jnp.float32)[:, None] * inv_freq
    return (jnp.cos(ar), jnp.sin(ar), jnp.cos(ac), jnp.sin(ac))


def _rope_rotate(x, cos, sin):
    x1, x2 = jnp.split(x, 2, axis=-1)
    cos = cos[None, :, None, :]
    sin = sin[None, :, None, :]
    return jnp.concatenate([x1 * cos - x2 * sin, x2 * cos + x1 * sin], axis=-1)


def apply_axial_rope(x, rope):
    cr, sr, cc, sc = rope
    xr, xcol = jnp.split(x, 2, axis=-1)
    y = jnp.concatenate([_rope_rotate(xr, cr, sr), _rope_rotate(xcol, cc, sc)], axis=-1)
    return y.astype(x.dtype)


def softmax_with_sink(s, sink):
    m = jnp.maximum(jnp.max(s, axis=-1, keepdims=True), sink)
    e = jnp.exp(s - m)
    return e / (jnp.sum(e, axis=-1, keepdims=True) + jnp.exp(sink - m))


def fourier_mix(u):
    b_, L, _ = u.shape
    ug = u.astype(jnp.float32).reshape(b_, L, FOURIER_GROUPS, FOURIER_GROUP_DIM)
    f = jnp.fft.fft2(ug, axes=(1, 3), norm='ortho').real
    return f.reshape(b_, L, BRANCH_WIDTH).astype(u.dtype)


def short_conv(gb, gc, v, w):
    u = gc * v
    L = u.shape[1]
    up = jnp.pad(u, ((0, 0), (1, 1), (0, 0)))
    y = up[:, 0:L] * w[0] + up[:, 1:L + 1] * w[1] + up[:, 2:L + 2] * w[2]
    return gb * y


def swa_latent(q, k, v, kc, vc, sink):
    b_, S, H, hd = q.shape
    G = H // SWA_KV_HEADS
    nb = S // SWA_BLOCK
    Lc = kc.shape[1]
    qb = q.reshape(b_, nb, SWA_BLOCK, SWA_KV_HEADS, G, hd)

    def windows(t):
        tp = jnp.pad(t, ((0, 0), (SWA_BLOCK, SWA_BLOCK), (0, 0), (0, 0)))
        tp = tp.reshape(b_, nb + 2, SWA_BLOCK, SWA_KV_HEADS, hd)
        return jnp.concatenate([tp[:, :-2], tp[:, 1:-1], tp[:, 2:]], axis=2)

    kw, vw = windows(k), windows(v)
    scale = hd ** -0.5
    s_loc = jnp.einsum('bnqhgd,bnkhd->bnhgqk', qb, kw).astype(jnp.float32) * scale
    s_ctx = jnp.einsum('bnqhgd,bkhd->bnhgqk', qb, kc).astype(jnp.float32) * scale
    blk = jnp.arange(nb)[:, None, None] * SWA_BLOCK
    qpos = blk + jnp.arange(SWA_BLOCK)[None, :, None]
    kpos = blk - SWA_BLOCK + jnp.arange(3 * SWA_BLOCK)[None, None, :]
    valid = (jnp.abs(qpos - kpos) <= SWA_WINDOW) & (kpos >= 0) & (kpos < S)
    s_loc = jnp.where(valid[None, :, None, None], s_loc, NEG_INF)
    sink_b = sink.astype(jnp.float32).reshape(SWA_KV_HEADS, G)[:, :, None, None]
    p = softmax_with_sink(jnp.concatenate([s_ctx, s_loc], axis=-1), sink_b).astype(v.dtype)
    o = (jnp.einsum('bnhgqk,bkhd->bnqhgd', p[..., :Lc], vc)
         + jnp.einsum('bnhgqk,bnkhd->bnqhgd', p[..., Lc:], vw))
    return o.reshape(b_, S, H * hd)


def swa_context(q, k, v, sink):
    b_, L, H, hd = q.shape
    G = H // SWA_KV_HEADS
    qg = q.reshape(b_, L, SWA_KV_HEADS, G, hd)
    s = jnp.einsum('bqhgd,bkhd->bhgqk', qg, k).astype(jnp.float32) * hd ** -0.5
    sink_b = sink.astype(jnp.float32).reshape(SWA_KV_HEADS, G)[:, :, None, None]
    p = softmax_with_sink(s, sink_b).astype(v.dtype)
    return jnp.einsum('bhgqk,bkhd->bqhgd', p, v).reshape(b_, L, H * hd)


def diff_core(q, k, v, lam):
    d = q.shape[-1]
    s = jnp.einsum('bqhcd,bkhcd->bhcqk', q, k).astype(jnp.float32) * d ** -0.5
    p = jax.nn.softmax(s, axis=-1)
    a = p[:, :, 0] - lam * p[:, :, 1]
    return jnp.einsum('bhqk,bkhd->bqhd', a.astype(v.dtype), v)


def diff_latent(q, k, v, kc, vc, lam):
    b_, S, H, _, d = q.shape
    nb = S // DIFF_BLOCK
    k_all = jnp.concatenate([kc, k], axis=1)
    v_all = jnp.concatenate([vc, v], axis=1)
    qb = jnp.moveaxis(q.reshape(b_, nb, DIFF_BLOCK, H, 2, d), 1, 0)
    ob = lax.map(lambda qi: diff_core(qi, k_all, v_all, lam), qb)
    return jnp.moveaxis(ob, 0, 1).reshape(b_, S, H, 2 * d)


def gated_merge(branches, z, w_branch, gate_up, w_out):
    merged = None
    for i, o in enumerate(branches):
        term = jax.nn.sigmoid(z @ gate_up[:, i]) * (o @ w_branch[i])
        merged = term if merged is None else merged + term
    return merged @ w_out


def token_mixers(h_c, h_x, need_ctx_out, rope_swa, rope_diff, lambda_init,
                 w_in, conv_w, swa_q_norm, swa_k_norm, swa_sink,
                 diff_q_norm, diff_k_norm, diff_lambda, diff_subln, w_branch, gate_up, w_out):
    cuts = tuple(int(v) for v in np.cumsum(IN_SPLITS)[:-1])
    pc = jnp.split(h_c @ w_in, cuts, axis=-1)
    px = jnp.split(h_x @ w_in, cuts, axis=-1)

    def heads(t, n, d):
        return t.reshape(t.shape[0], t.shape[1], n, d)

    def swa_q(p, rope):
        q = rms_norm(heads(p[4], SWA_HEADS, SWA_HEAD_DIM), swa_q_norm)
        return q if rope is None else apply_axial_rope(q, rope)

    def swa_kv(p, rope):
        k = rms_norm(heads(p[5], SWA_KV_HEADS, SWA_HEAD_DIM), swa_k_norm)
        k = k if rope is None else apply_axial_rope(k, rope)
        return k, heads(p[6], SWA_KV_HEADS, SWA_HEAD_DIM)

    def diff_qk(t, g, rope):
        b_, L = t.shape[0], t.shape[1]
        u = rms_norm(t.reshape(b_, L, 2 * DIFF_HEADS, DIFF_HEAD_DIM), g)
        if rope is not None:
            u = apply_axial_rope(u, rope)
        return u.reshape(b_, L, DIFF_HEADS, 2, DIFF_HEAD_DIM)

    def diff_v(p):
        return heads(p[9], DIFF_HEADS, 2 * DIFF_HEAD_DIM)

    def diff_out(o):
        o = rms_norm(o, diff_subln) * (1.0 - lambda_init)
        return o.reshape(o.shape[0], o.shape[1], BRANCH_WIDTH)

    dl = diff_lambda.astype(jnp.float32)
    lam = jnp.exp(jnp.sum(dl[0] * dl[1])) - jnp.exp(jnp.sum(dl[2] * dl[3])) + lambda_init

    ks_c, vs_c = swa_kv(pc, None)
    kd_c, vd_c = diff_qk(pc[8], diff_k_norm, None), diff_v(pc)

    ks_x, vs_x = swa_kv(px, rope_swa)
    out_x = gated_merge([
        fourier_mix(px[0]),
        short_conv(px[1], px[2], px[3], conv_w),
        swa_latent(swa_q(px, rope_swa), ks_x, vs_x, ks_c, vs_c, swa_sink),
        diff_out(diff_latent(diff_qk(px[7], diff_q_norm, rope_diff),
                             diff_qk(px[8], diff_k_norm, rope_diff),
                             diff_v(px), kd_c, vd_c, lam)),
    ], px[10], w_branch, gate_up, w_out)
    if not need_ctx_out:
        return out_x, None
    out_c = gated_merge([
        fourier_mix(pc[0]),
        short_conv(pc[1], pc[2], pc[3], conv_w),
        swa_context(swa_q(pc, None), ks_c, vs_c, swa_sink),
        diff_out(diff_core(diff_qk(pc[7], diff_q_norm, None), kd_c, vd_c, lam)),
    ], pc[10], w_branch, gate_up, w_out)
    return out_x, out_c


def setup_inputs(seed: int = 0) -> dict:
    key = jax.random.key(seed)
    ks = jax.random.split(key, 22)
    D = D_MODEL

    def nrm(k, shape, scale):
        return jax.random.normal(k, shape, jnp.float32) * scale

    return {
        'x': nrm(ks[0], (BATCH, SEQ, D), 1.0),
        'c': nrm(ks[1], (BATCH, D), 1.0),
        'ctx': nrm(ks[2], (BATCH, CTX_LEN, D), 1.0),
        'c_ctx': nrm(ks[3], (D,), 1.0),
        'norm_g': 1.0 + nrm(ks[4], (DEPTH, 3, D), 0.02),
        'ada_down': nrm(ks[5], (DEPTH, D, ADA_RANK), D ** -0.5),
        'ada_up': nrm(ks[6], (DEPTH, ADA_RANK, N_MOD * D), 0.5 * ADA_RANK ** -0.5),
        'ada_b': nrm(ks[7], (DEPTH, N_MOD * D), 0.02),
        'ffn_w13': nrm(ks[8], (DEPTH, 2, D, 2 * D_FF), D ** -0.5),
        'ffn_w2': nrm(ks[9], (DEPTH, 2, D_FF, D), D_FF ** -0.5),
        'w_in': nrm(ks[10], (DEPTH, D, IN_COLS), D ** -0.5),
        'conv_w': nrm(ks[11], (DEPTH, CONV_K, BRANCH_WIDTH), CONV_K ** -0.5),
        'swa_q_norm': 1.0 + nrm(ks[12], (DEPTH, SWA_HEAD_DIM), 0.02),
        'swa_k_norm': 1.0 + nrm(ks[13], (DEPTH, SWA_HEAD_DIM), 0.02),
        'swa_sink': nrm(ks[14], (DEPTH, SWA_HEADS), 0.5),
        'diff_q_norm': 1.0 + nrm(ks[15], (DEPTH, DIFF_HEAD_DIM), 0.02),
        'diff_k_norm': 1.0 + nrm(ks[16], (DEPTH, DIFF_HEAD_DIM), 0.02),
        'diff_lambda': nrm(ks[17], (DEPTH, 4, DIFF_HEAD_DIM), 0.1),
        'diff_subln': 1.0 + nrm(ks[18], (DEPTH, 2 * DIFF_HEAD_DIM), 0.02),
        'w_branch': nrm(ks[19], (DEPTH, N_BRANCHES, BRANCH_WIDTH, D), BRANCH_WIDTH ** -0.5),
        'gate_up': nrm(ks[20], (DEPTH, GATE_RANK, N_BRANCHES, D), GATE_RANK ** -0.5),
        'w_out': nrm(ks[21], (DEPTH, D, D), D ** -0.5),
    }


def reference(x, c, ctx, c_ctx, norm_g, ada_down, ada_up, ada_b, ffn_w13, ffn_w2,
              w_in, conv_w, swa_q_norm, swa_k_norm, swa_sink, diff_q_norm, diff_k_norm,
              diff_lambda, diff_subln, w_branch, gate_up, w_out):
    n_lat = x.shape[1]
    rows = n_lat // GRID_W
    row = jnp.repeat(jnp.arange(rows, dtype=jnp.int32), GRID_W)
    col = jnp.tile(jnp.arange(GRID_W, dtype=jnp.int32), rows)
    rope_swa = axial_rope_tables(row, col, SWA_HEAD_DIM)
    rope_diff = axial_rope_tables(row, col, DIFF_HEAD_DIM)

    xc = ctx
    for l in range(DEPTH):
        last = l == DEPTH - 1
        mx = ada_modulation(c, ada_down[l], ada_up[l], ada_b[l])
        mc = ada_modulation(c_ctx, ada_down[l], ada_up[l], ada_b[l])
        x = ffn_half_step(x, mx[0], mx[1], mx[2], norm_g[l, 0], ffn_w13[l, 0], ffn_w2[l, 0])
        xc = ffn_half_step(xc, mc[0], mc[1], mc[2], norm_g[l, 0], ffn_w13[l, 0], ffn_w2[l, 0])
        hx = modulate(rms_norm(x, norm_g[l, 1]), mx[3], mx[4])
        hc = modulate(rms_norm(xc, norm_g[l, 1]), mc[3], mc[4])
        lambda_init = 0.8 - 0.6 * math.exp(-0.3 * l)
        out_x, out_c = token_mixers(hc, hx, not last, rope_swa, rope_diff, lambda_init,
                                    w_in[l], conv_w[l], swa_q_norm[l], swa_k_norm[l], swa_sink[l],
                                    diff_q_norm[l], diff_k_norm[l], diff_lambda[l], diff_subln[l],
                                    w_branch[l], gate_up[l], w_out[l])
        x = x + mx[5] * out_x
        if not last:
            xc = xc + mc[5] * out_c
            xc = ffn_half_step(xc, mc[6], mc[7], mc[8], norm_g[l, 2], ffn_w13[l, 1], ffn_w2[l, 1])
        x = ffn_half_step(x, mx[6], mx[7], mx[8], norm_g[l, 2], ffn_w13[l, 1], ffn_w2[l, 1])
    return x
```

```python
import functools
import math

import jax
import jax.numpy as jnp
from jax import lax
from jax.experimental import pallas as pl
from jax.experimental.pallas import tpu as pltpu

F32 = jnp.float32
BF16 = jnp.bfloat16

EPS = 1e-6
NEG_INF = -1e30
ROPE_BASE = 10000.0
GRID_W = 64
N_MOD = 9
FOURIER_GROUPS = 4
SWA_HEADS = 8
SWA_KV_HEADS = 2
SWA_BLOCK = 128
SWA_WINDOW = 128
DIFF_HEADS = 8
LANES = 128
SUBLANES = 8
FFT_INNER = 128

VMEM_LIMIT_BYTES = 56 * 1024 * 1024
ROW_TILE = 512
ELT_ROW_TILE = 256

_HIGHEST = lax.Precision.HIGHEST
_NT_DIMS = (((1,), (1,)), ((), ()))


def _params(*sem):
    return pltpu.CompilerParams(dimension_semantics=sem, vmem_limit_bytes=VMEM_LIMIT_BYTES)


def _dot(a, b):
    return jnp.dot(a, b, preferred_element_type=F32)


def _ada_kernel(cv_ref, down_ref, up_ref, b_ref, o_ref, hid_ref):
    @pl.when(pl.program_id(1) == 0)
    def _():
        c = cv_ref[...]
        s = c * jax.nn.sigmoid(c)
        hid_ref[...] = jnp.dot(s, down_ref[...], precision=_HIGHEST, preferred_element_type=F32)

    o_ref[...] = jnp.dot(hid_ref[...], up_ref[...], precision=_HIGHEST,
                         preferred_element_type=F32) + b_ref[...]


def _ada(cv, down, up, b):
    depth, d, r = down.shape
    n = up.shape[2]
    tn = n // N_MOD
    return pl.pallas_call(
        _ada_kernel,
        grid=(depth, n // tn),
        in_specs=[pl.BlockSpec((SUBLANES, d), lambda l, j: (0, 0)),
                  pl.BlockSpec((None, d, r), lambda l, j: (l, 0, 0)),
                  pl.BlockSpec((None, r, tn), lambda l, j: (l, 0, j)),
                  pl.BlockSpec((None, 1, tn), lambda l, j: (l, 0, j))],
        out_specs=pl.BlockSpec((None, SUBLANES, tn), lambda l, j: (l, 0, j)),
        out_shape=jax.ShapeDtypeStruct((depth, SUBLANES, n), F32),
        scratch_shapes=[pltpu.VMEM((SUBLANES, r), F32)],
        compiler_params=_params("parallel", "arbitrary"),
        name="ada_modulation",
    )(cv, down, up, b.reshape(depth, 1, n))


def _normmod_kernel(t_ref, g_ref, sh_ref, sc_ref, o_ref):
    x = t_ref[...]
    ms = jnp.mean(x * x, axis=-1, keepdims=True)
    y = x * lax.rsqrt(ms + EPS) * g_ref[...]
    o_ref[...] = (y * (1.0 + sc_ref[...]) + sh_ref[...]).astype(o_ref.dtype)


def _normmod(t, gains, g_row, mods, mod_row, k_shift, k_scale, tr):
    m, d = t.shape
    vec = lambda idx: pl.BlockSpec((None, 1, d), idx)
    return pl.pallas_call(
        _normmod_kernel,
        grid=(m // tr,),
        in_specs=[pl.BlockSpec((tr, d), lambda i: (i, 0)),
                  vec(lambda i: (g_row, 0, 0)),
                  vec(lambda i: (mod_row(i, tr) + k_shift, 0, 0)),
                  vec(lambda i: (mod_row(i, tr) + k_scale, 0, 0))],
        out_specs=pl.BlockSpec((tr, d), lambda i: (i, 0)),
        out_shape=jax.ShapeDtypeStruct((m, d), BF16),
        compiler_params=_params("parallel"),
        name="norm_modulate",
    )(t, gains, mods, mods)


def _mm_kernel(x_ref, w_ref, o_ref):
    o_ref[...] = _dot(x_ref[...], w_ref[...]).astype(o_ref.dtype)


def _w_spec(widx, k, tn, col_block):
    lead = (None,) * len(widx)
    return pl.BlockSpec(lead + (k, tn), lambda i, j: widx + (0, col_block + j))


def _mm(x, w, widx, col_off, ncols, tn, out_dtype, tm):
    m, k = x.shape
    return pl.pallas_call(
        _mm_kernel,
        grid=(m // tm, ncols // tn),
        in_specs=[pl.BlockSpec((tm, k), lambda i, j: (i, 0)),
                  _w_spec(widx, k, tn, col_off // tn)],
        out_specs=pl.BlockSpec((tm, tn), lambda i, j: (i, j)),
        out_shape=jax.ShapeDtypeStruct((m, ncols), out_dtype),
        compiler_params=_params("parallel", "arbitrary"),
        name="matmul",
    )(x, w)


def _mm_swiglu_kernel(x_ref, wa_ref, wu_ref, o_ref):
    x = x_ref[...]
    a = _dot(x, wa_ref[...])
    u = _dot(x, wu_ref[...])
    o_ref[...] = (a * jax.nn.sigmoid(a) * u).astype(o_ref.dtype)


def _mm_swiglu(x, w13, widx, tn, tm):
    m, k = x.shape
    f = w13.shape[-1] // 2
    return pl.pallas_call(
        _mm_swiglu_kernel,
        grid=(m // tm, f // tn),
        in_specs=[pl.BlockSpec((tm, k), lambda i, j: (i, 0)),
                  _w_spec(widx, k, tn, 0),
                  _w_spec(widx, k, tn, f // tn)],
        out_specs=pl.BlockSpec((tm, tn), lambda i, j: (i, j)),
        out_shape=jax.ShapeDtypeStruct((m, f), BF16),
        compiler_params=_params("parallel", "arbitrary"),
        name="matmul_swiglu",
    )(x, w13, w13)


def _mm_resid_kernel(x_ref, w_ref, t_ref, g_ref, o_ref, *, coef):
    y = _dot(x_ref[...], w_ref[...])
    o_ref[...] = t_ref[...] + (coef * g_ref[...]) * y


def _mm_resid(x, w, widx, t, mods, mod_row, k_gate, coef, tn, tm, m_rows):
    k = x.shape[1]
    d = t.shape[1]
    return pl.pallas_call(
        functools.partial(_mm_resid_kernel, coef=coef),
        grid=(m_rows // tm, d // tn),
        in_specs=[pl.BlockSpec((tm, k), lambda i, j: (i, 0)),
                  _w_spec(widx, k, tn, 0),
                  pl.BlockSpec((tm, tn), lambda i, j: (i, j)),
                  pl.BlockSpec((None, 1, tn), lambda i, j: (mod_row(i, tm) + k_gate, 0, j))],
        out_specs=pl.BlockSpec((tm, tn), lambda i, j: (i, j)),
        out_shape=jax.ShapeDtypeStruct((m_rows, d), F32),
        compiler_params=_params("parallel", "arbitrary"),
        name="matmul_gated_residual",
    )(x, w, t, mods)


def _merge_kernel(o0_ref, o1_ref, o2_ref, o3_ref, z_ref, wb_ref, g0_ref, g1_ref, g2_ref, g3_ref, out_ref):
    z = z_ref[...]
    acc = None
    for i, (o_ref, g_ref) in enumerate(((o0_ref, g0_ref), (o1_ref, g1_ref), (o2_ref, g2_ref), (o3_ref, g3_ref))):
        term = jax.nn.sigmoid(_dot(z, g_ref[...])) * _dot(o_ref[...], wb_ref[i])
        acc = term if acc is None else acc + term
    out_ref[...] = acc.astype(out_ref.dtype)


def _merge(branches, z, w_branch, gate_up2, layer, tn, tm, m_rows):
    bw = branches[0].shape[1]
    r = z.shape[1]
    nbr, d = w_branch.shape[1], w_branch.shape[3]
    row = lambda width: pl.BlockSpec((tm, width), lambda i, j: (i, 0))
    gate = lambda b: pl.BlockSpec((None, r, tn), lambda i, j: (layer, 0, b * (d // tn) + j))
    return pl.pallas_call(
        _merge_kernel,
        grid=(m_rows // tm, d // tn),
        in_specs=[row(bw)] * nbr + [row(r),
                  pl.BlockSpec((None, nbr, bw, tn), lambda i, j: (layer, 0, 0, j))]
                 + [gate(b) for b in range(nbr)],
        out_specs=pl.BlockSpec((tm, tn), lambda i, j: (i, j)),
        out_shape=jax.ShapeDtypeStruct((m_rows, d), BF16),
        compiler_params=_params("parallel", "arbitrary"),
        name="gated_merge",
    )(*branches, z, w_branch, *([gate_up2] * nbr))


def _prep_kernel(x_ref, g_ref, cos_ref, sin_ref, o_ref, *, seg, half, n_rope, n_scaled, qscale):
    j = pl.program_id(1)
    width = x_ref.shape[1]

    @pl.when(j < n_rope)
    def _():
        scale = jnp.where(j < n_scaled, qscale, 1.0).astype(F32)
        cos = cos_ref[...]
        sin = sin_ref[...]
        for c in range(width // LANES):
            sl = slice(c * LANES, (c + 1) * LANES)
            x = x_ref[:, sl]
            lane = lax.broadcasted_iota(jnp.int32, x.shape, 1)
            ss = x * x
            if seg == LANES:
                ms = jnp.mean(ss, axis=-1, keepdims=True)
            else:
                lo = lane < seg
                s_lo = jnp.sum(jnp.where(lo, ss, 0.0), axis=-1, keepdims=True)
                s_hi = jnp.sum(jnp.where(lo, 0.0, ss), axis=-1, keepdims=True)
                ms = jnp.where(lo, s_lo, s_hi) * (1.0 / seg)
            xn = x * lax.rsqrt(ms + EPS) * g_ref[:, sl]
            lower = pltpu.roll(xn, half, 1)
            upper = pltpu.roll(xn, LANES - half, 1)
            partner = jnp.where((lane & half) == 0, upper, lower)
            y = xn * cos + partner * sin
            o_ref[:, sl] = (y * scale).astype(o_ref.dtype)

    @pl.when(j >= n_rope)
    def _():
        o_ref[...] = x_ref[...].astype(o_ref.dtype)


def _prep(p, gains, cos_t, sin_t, width, seg, half, n_rope, n_scaled, qscale, rope_row, tr):
    m, n = p.shape
    return pl.pallas_call(
        functools.partial(_prep_kernel, seg=seg, half=half, n_rope=n_rope, n_scaled=n_scaled, qscale=qscale),
        grid=(m // tr, n // width),
        in_specs=[pl.BlockSpec((tr, width), lambda i, j: (i, j)),
                  pl.BlockSpec((None, 1, width), lambda i, j: (j, 0, 0)),
                  pl.BlockSpec((tr, LANES), lambda i, j: (rope_row(i, tr), 0)),
                  pl.BlockSpec((tr, LANES), lambda i, j: (rope_row(i, tr), 0))],
        out_specs=pl.BlockSpec((tr, width), lambda i, j: (i, j)),
        out_shape=jax.ShapeDtypeStruct((m, n), BF16),
        compiler_params=_params("parallel", "arbitrary"),
        name="qk_norm_rope",
    )(p, gains, cos_t, sin_t)


def _rope_tables(seq, head_dim, ident_rows):
    rows = seq // GRID_W
    row = jnp.repeat(jnp.arange(rows, dtype=jnp.int32), GRID_W)
    col = jnp.tile(jnp.arange(GRID_W, dtype=jnp.int32), rows)
    axis_dim = head_dim // 2
    inv_freq = ROPE_BASE ** (-jnp.arange(0, axis_dim, 2, dtype=F32) / axis_dim)
    ar = row.astype(F32)[:, None] * inv_freq
    ac = col.astype(F32)[:, None] * inv_freq
    cr, sr, cc, sc = jnp.cos(ar), jnp.sin(ar), jnp.cos(ac), jnp.sin(ac)
    cos_t = jnp.concatenate([cr, cr, cc, cc], axis=-1)
    sin_t = jnp.concatenate([-sr, sr, -sc, sc], axis=-1)
    reps = LANES // head_dim
    cos_t = jnp.tile(cos_t, (1, reps))
    sin_t = jnp.tile(sin_t, (1, reps))
    cos_t = jnp.concatenate([cos_t, jnp.ones((ident_rows, LANES), F32)], axis=0)
    sin_t = jnp.concatenate([sin_t, jnp.zeros((ident_rows, LANES), F32)], axis=0)
    return cos_t, sin_t


def _conv_kernel(gb_ref, gc_ref, v_ref, gcp_ref, vp_ref, gcn_ref, vn_ref, w_ref, o_ref, *, seq, ctx_len, n_lat):
    tr = gb_ref.shape[0]
    i = pl.program_id(0)
    u = gc_ref[...] * v_ref[...]
    u_before = (gcp_ref[...] * vp_ref[...])[SUBLANES - 1:SUBLANES, :]
    u_after = (gcn_ref[...] * vn_ref[...])[0:1, :]
    r = lax.broadcasted_iota(jnp.int32, (tr, 1), 0)
    gr = i * tr + r
    is_lat = gr < n_lat
    pos = jnp.where(is_lat, gr & (seq - 1), (gr - n_lat) & (ctx_len - 1))
    last = jnp.where(is_lat, seq - 1, ctx_len - 1)
    u_m = jnp.where(r == 0, u_before, pltpu.roll(u, 1, 0))
    u_m = jnp.where(pos == 0, 0.0, u_m)
    u_p = jnp.where(r == tr - 1, u_after, pltpu.roll(u, tr - 1, 0))
    u_p = jnp.where(pos == last, 0.0, u_p)
    y = u_m * w_ref[0:1, :] + u * w_ref[1:2, :] + u_p * w_ref[2:3, :]
    o_ref[...] = (gb_ref[...] * y).astype(o_ref.dtype)


def _conv(pcv, conv_w, layer, seq, ctx_len, n_lat, tr):
    m = pcv.shape[0]
    bw = pcv.shape[1] // 3
    halo = tr // SUBLANES
    n_halo = m // SUBLANES
    main = lambda c: pl.BlockSpec((tr, bw), lambda i: (i, c))
    before = lambda c: pl.BlockSpec((SUBLANES, bw), lambda i: (jnp.maximum(i * halo - 1, 0), c))
    after = lambda c: pl.BlockSpec((SUBLANES, bw), lambda i: (jnp.minimum((i + 1) * halo, n_halo - 1), c))
    return pl.pallas_call(
        functools.partial(_conv_kernel, seq=seq, ctx_len=ctx_len, n_lat=n_lat),
        grid=(m // tr,),
        in_specs=[main(0), main(1), main(2), before(1), before(2), after(1), after(2),
                  pl.BlockSpec((None, 3, bw), lambda i: (layer, 0, 0))],
        out_specs=pl.BlockSpec((tr, bw), lambda i: (i, 0)),
        out_shape=jax.ShapeDtypeStruct((m, bw), BF16),
        compiler_params=_params("parallel"),
        name="short_conv",
    )(pcv, pcv, pcv, pcv, pcv, pcv, pcv, conv_w)


def _sink_column(sink_ref, h, groups, blk):
    r = lax.broadcasted_iota(jnp.int32, (groups * blk, 1), 0)
    col = jnp.zeros((groups * blk, 1), F32)
    for g in range(groups):
        col = jnp.where((r >= g * blk) & (r < (g + 1) * blk), sink_ref[h * groups + g], col)
    return col


def _stack_heads(q, groups, hd):
    return jnp.concatenate([q[:, g * hd:(g + 1) * hd] for g in range(groups)], axis=0)


def _swa_lat_kernel(sink_ref, q_ref, kp_ref, kc_ref, kn_ref, vp_ref, vc_ref, vn_ref, kx_ref, vx_ref, o_ref,
                    *, seq, groups, hd, scale):
    n = pl.program_id(1)
    h = pl.program_id(2)
    blk = q_ref.shape[0]
    q = _stack_heads(q_ref[...], groups, hd)
    kw = jnp.concatenate([kp_ref[...], kc_ref[...], kn_ref[...]], axis=0)
    vw = jnp.concatenate([vp_ref[...], vc_ref[...], vn_ref[...]], axis=0)
    s_loc = lax.dot_general(q, kw, _NT_DIMS, preferred_element_type=F32) * scale
    s_ctx = lax.dot_general(q, kx_ref[...], _NT_DIMS, preferred_element_type=F32) * scale
    ri = lax.broadcasted_iota(jnp.int32, s_loc.shape, 0)
    ci = lax.broadcasted_iota(jnp.int32, s_loc.shape, 1)
    rel = ci - blk - (ri & (blk - 1))
    kpos = (n - 1) * blk + ci
    valid = (jnp.abs(rel) <= SWA_WINDOW) & (kpos >= 0) & (kpos < seq)
    s_loc = jnp.where(valid, s_loc, NEG_INF)
    sink = _sink_column(sink_ref, h, groups, blk)
    m = jnp.maximum(jnp.maximum(jnp.max(s_ctx, axis=-1, keepdims=True),
                                jnp.max(s_loc, axis=-1, keepdims=True)), sink)
    e_ctx = jnp.exp(s_ctx - m)
    e_loc = jnp.exp(s_loc - m)
    den = (jnp.sum(e_ctx, axis=-1, keepdims=True) + jnp.sum(e_loc, axis=-1, keepdims=True)
           + jnp.exp(sink - m))
    o = (_dot(e_ctx.astype(BF16), vx_ref[...]) + _dot(e_loc.astype(BF16), vw)) / den
    for g in range(groups):
        o_ref[:, g * hd:(g + 1) * hd] = o[g * blk:(g + 1) * blk].astype(o_ref.dtype)


def _swa_ctx_kernel(sink_ref, q_ref, kx_ref, vx_ref, o_ref, *, groups, hd, scale):
    h = pl.program_id(1)
    blk = q_ref.shape[0]
    q = _stack_heads(q_ref[...], groups, hd)
    s = lax.dot_general(q, kx_ref[...], _NT_DIMS, preferred_element_type=F32) * scale
    sink = _sink_column(sink_ref, h, groups, blk)
    m = jnp.maximum(jnp.max(s, axis=-1, keepdims=True), sink)
    e = jnp.exp(s - m)
    den = jnp.sum(e, axis=-1, keepdims=True) + jnp.exp(sink - m)
    o = _dot(e.astype(BF16), vx_ref[...]) / den
    for g in range(groups):
        o_ref[:, g * hd:(g + 1) * hd] = o[g * blk:(g + 1) * blk].astype(o_ref.dtype)


def _swa(qkv, sink, hd, batch, seq, ctx_len):
    bw = SWA_HEADS * hd
    groups = SWA_HEADS // SWA_KV_HEADS
    gw = groups * hd
    scale = hd ** -0.5
    blk = SWA_BLOCK
    nb = seq // blk
    kcol = bw // hd
    vcol = kcol + SWA_KV_HEADS
    ctx_row = batch * seq // ctx_len
    smem = pl.BlockSpec(memory_space=pltpu.SMEM)

    def kv(col, shift):
        def idx(b, n, h):
            return (b * nb + jnp.clip(n + shift, 0, nb - 1), col + h)
        return pl.BlockSpec((blk, hd), idx)

    ctx_kv = lambda col: pl.BlockSpec((ctx_len, hd), lambda b, n, h: (ctx_row + b, col + h))
    lat = pl.pallas_call(
        functools.partial(_swa_lat_kernel, seq=seq, groups=groups, hd=hd, scale=scale),
        grid=(batch, nb, SWA_KV_HEADS),
        in_specs=[smem, pl.BlockSpec((blk, gw), lambda b, n, h: (b * nb + n, h)),
                  kv(kcol, -1), kv(kcol, 0), kv(kcol, 1), kv(vcol, -1), kv(vcol, 0), kv(vcol, 1),
                  ctx_kv(kcol), ctx_kv(vcol)],
        out_specs=pl.BlockSpec((blk, gw), lambda b, n, h: (b * nb + n, h)),
        out_shape=jax.ShapeDtypeStruct((batch * seq, bw), BF16),
        compiler_params=_params("parallel", "parallel", "parallel"),
        name="swa_latent",
    )(sink, qkv, qkv, qkv, qkv, qkv, qkv, qkv, qkv, qkv)
    ctx_kv2 = lambda col: pl.BlockSpec((ctx_len, hd), lambda b, h: (ctx_row + b, col + h))
    ctx = pl.pallas_call(
        functools.partial(_swa_ctx_kernel, groups=groups, hd=hd, scale=scale),
        grid=(batch, SWA_KV_HEADS),
        in_specs=[smem, pl.BlockSpec((ctx_len, gw), lambda b, h: (ctx_row + b, h)),
                  ctx_kv2(kcol), ctx_kv2(vcol)],
        out_specs=pl.BlockSpec((ctx_len, gw), lambda b, h: (b, h)),
        out_shape=jax.ShapeDtypeStruct((batch * ctx_len, bw), BF16),
        compiler_params=_params("parallel", "parallel"),
        name="swa_context",
    )(sink, qkv, qkv, qkv)
    return lat, ctx


def _diff_kernel(dl_ref, g_ref, q_ref, kx_ref, vx_ref, *rest, lam_init, tk, with_latent):
    if with_latent:
        k_ref, v_ref, o_ref = rest
    else:
        (o_ref,) = rest
    tq, hd2 = q_ref.shape
    hd = hd2 // 2
    qf = q_ref[...].astype(F32)
    lane = lax.broadcasted_iota(jnp.int32, qf.shape, 1)
    q0 = jnp.where(lane < hd, qf, 0.0).astype(BF16)
    q1 = jnp.where(lane < hd, 0.0, qf).astype(BF16)

    def update(s, v, m, l, acc):
        m_new = jnp.maximum(m, jnp.max(s, axis=-1, keepdims=True))
        alpha = jnp.exp(m - m_new)
        p = jnp.exp(s - m_new)
        l = alpha * l + jnp.sum(p, axis=-1, keepdims=True)
        acc = alpha * acc + _dot(p.astype(BF16), v)
        return m_new, l, acc

    def chunk(k, v, carry):
        s0 = lax.dot_general(q0, k, _NT_DIMS, preferred_element_type=F32)
        s1 = lax.dot_general(q1, k, _NT_DIMS, preferred_element_type=F32)
        return update(s0, v, *carry[:3]) + update(s1, v, *carry[3:])

    init = (jnp.full((tq, 1), NEG_INF, F32), jnp.zeros((tq, 1), F32), jnp.zeros((tq, hd2), F32))
    carry = chunk(kx_ref[...], vx_ref[...], init + init)
    if with_latent:
        def body(c, carry):
            rows = pl.ds(pl.multiple_of(c * tk, tk), tk)
            return chunk(k_ref[rows, :], v_ref[rows, :], carry)
        carry = lax.fori_loop(0, k_ref.shape[0] // tk, body, carry)
    _, l0, a0, _, l1, a1 = carry
    dl = dl_ref[...]
    lam = (jnp.exp(jnp.sum(dl[0:1] * dl[1:2], axis=-1, keepdims=True))
           - jnp.exp(jnp.sum(dl[2:3] * dl[3:4], axis=-1, keepdims=True)) + lam_init)
    o = a0 / l0 - lam * (a1 / l1)
    y = o * lax.rsqrt(jnp.mean(o * o, axis=-1, keepdims=True) + EPS) * g_ref[...]
    o_ref[...] = (y * (1.0 - lam_init)).astype(o_ref.dtype)


def _diff(qkv, diff_lambda, subln, layer, lam_init, batch, seq, ctx_len, tq, tk):
    bw = qkv.shape[1] // 3
    hd2 = bw // DIFF_HEADS
    kcol = DIFF_HEADS
    vcol = 2 * DIFF_HEADS
    ctx_row = batch * seq // ctx_len
    nq = seq // tq
    dl_spec3 = pl.BlockSpec((None,) + diff_lambda.shape[1:], lambda b, h, i: (layer, 0, 0))
    g_spec3 = pl.BlockSpec((None, 1, hd2), lambda b, h, i: (layer, 0, 0))
    lat = pl.pallas_call(
        functools.partial(_diff_kernel, lam_init=lam_init, tk=tk, with_latent=True),
        grid=(batch, DIFF_HEADS, nq),
        in_specs=[dl_spec3, g_spec3,
                  pl.BlockSpec((tq, hd2), lambda b, h, i: (b * nq + i, h)),
                  pl.BlockSpec((ctx_len, hd2), lambda b, h, i: (ctx_row + b, kcol + h)),
                  pl.BlockSpec((ctx_len, hd2), lambda b, h, i: (ctx_row + b, vcol + h)),
                  pl.BlockSpec((seq, hd2), lambda b, h, i: (b, kcol + h)),
                  pl.BlockSpec((seq, hd2), lambda b, h, i: (b, vcol + h))],
        out_specs=pl.BlockSpec((tq, hd2), lambda b, h, i: (b * nq + i, h)),
        out_shape=jax.ShapeDtypeStruct((batch * seq, bw), BF16),
        compiler_params=_params("parallel", "parallel", "arbitrary"),
        name="diff_latent",
    )(diff_lambda, subln, qkv, qkv, qkv, qkv, qkv)
    dl_spec2 = pl.BlockSpec((None,) + diff_lambda.shape[1:], lambda b, h: (layer, 0, 0))
    g_spec2 = pl.BlockSpec((None, 1, hd2), lambda b, h: (layer, 0, 0))
    ctx = pl.pallas_call(
        functools.partial(_diff_kernel, lam_init=lam_init, tk=tk, with_latent=False),
        grid=(batch, DIFF_HEADS),
        in_specs=[dl_spec2, g_spec2,
                  pl.BlockSpec((ctx_len, hd2), lambda b, h: (ctx_row + b, h)),
                  pl.BlockSpec((ctx_len, hd2), lambda b, h: (ctx_row + b, kcol + h)),
                  pl.BlockSpec((ctx_len, hd2), lambda b, h: (ctx_row + b, vcol + h))],
        out_specs=pl.BlockSpec((ctx_len, hd2), lambda b, h: (b, h)),
        out_shape=jax.ShapeDtypeStruct((batch * ctx_len, bw), BF16),
        compiler_params=_params("parallel", "parallel"),
        name="diff_context",
    )(diff_lambda, subln, qkv, qkv, qkv)
    return lat, ctx


def _f1_kernel(f_ref, u_ref, a_ref):
    l1 = u_ref.shape[0]
    r = _dot(f_ref[...], u_ref[...])
    a_ref[0] = r[:l1].astype(a_ref.dtype)
    a_ref[1] = r[l1:].astype(a_ref.dtype)


def _f2_kernel(g_ref, a_ref, y_ref):
    n = a_ref.shape[1]
    r = _dot(g_ref[:, :n], a_ref[0]) + _dot(g_ref[:, n:], a_ref[1])
    y_ref[0] = r[:n].astype(y_ref.dtype)
    y_ref[1] = r[n:].astype(y_ref.dtype)


def _f3_kernel(yr_ref, yi_ref, c_ref, s_ref, o_ref):
    o_ref[...] = (_dot(yr_ref[...], c_ref[...]) + _dot(yi_ref[...], s_ref[...])).astype(o_ref.dtype)


def _fctx_kernel(f_ref, u_ref, c_ref, s_ref, o_ref):
    n = u_ref.shape[0]
    w = _dot(f_ref[...], u_ref[...])
    wr = w[:n].astype(BF16)
    wi = w[n:].astype(BF16)
    o_ref[...] = (_dot(wr, c_ref[...]) + _dot(wi, s_ref[...])).astype(o_ref.dtype)


def _angle(prod, n):
    return (2.0 * math.pi / n) * (prod % n).astype(F32)


def _dft_pair(n):
    k = jnp.arange(n, dtype=jnp.int32)
    ang = _angle(k[:, None] * k[None, :], n)
    return jnp.concatenate([jnp.cos(ang), -jnp.sin(ang)], axis=0)


def _fourier_tables(seq, ctx_len, gdim):
    l1 = seq // FFT_INNER
    f1 = _dft_pair(l1).astype(BF16)
    k1 = jnp.arange(l1, dtype=jnp.int32)[:, None, None]
    k2 = jnp.arange(FFT_INNER, dtype=jnp.int32)[None, :, None]
    t2 = jnp.arange(FFT_INNER, dtype=jnp.int32)[None, None, :]
    ang = _angle(t2 * (k1 + l1 * k2), seq)
    c, s = jnp.cos(ang), jnp.sin(ang)
    g2 = jnp.concatenate([jnp.concatenate([c, s], axis=2), jnp.concatenate([-s, c], axis=2)], axis=1).astype(BF16)
    kc = jnp.arange(gdim, dtype=jnp.int32)
    angc = _angle(kc[:, None] * kc[None, :], gdim)
    cc, sc = jnp.cos(angc), jnp.sin(angc)
    nl = 1.0 / math.sqrt(seq * gdim)
    nc = 1.0 / math.sqrt(ctx_len * gdim)
    return dict(f1=f1, g2=g2, c3=(cc * nl).astype(BF16), s3=(sc * nl).astype(BF16),
                fc=_dft_pair(ctx_len).astype(BF16), c3c=(cc * nc).astype(BF16), s3c=(sc * nc).astype(BF16))


def _fourier(pf, tabs, batch, seq, ctx_len):
    m, bw = pf.shape
    gdim = bw // FOURIER_GROUPS
    l1 = seq // FFT_INNER
    flat = FFT_INNER * bw
    cc1 = min(flat, 8192)
    a = pl.pallas_call(
        _f1_kernel,
        grid=(batch, flat // cc1),
        in_specs=[pl.BlockSpec((2 * l1, l1), lambda b, j: (0, 0)),
                  pl.BlockSpec((l1, cc1), lambda b, j: (b, j))],
        out_specs=pl.BlockSpec((None, 2, l1, cc1), lambda b, j: (b, 0, 0, j)),
        out_shape=jax.ShapeDtypeStruct((batch, 2, l1, flat), BF16),
        compiler_params=_params("parallel", "parallel"),
        name="fourier_stage1",
    )(tabs["f1"], pf.reshape(m // FFT_INNER, flat))
    y = pl.pallas_call(
        _f2_kernel,
        grid=(batch, l1),
        in_specs=[pl.BlockSpec((None, 2 * FFT_INNER, 2 * FFT_INNER), lambda b, k: (k, 0, 0)),
                  pl.BlockSpec((None, 2, None, FFT_INNER, bw), lambda b, k: (b, 0, k, 0, 0))],
        out_specs=pl.BlockSpec((None, 2, FFT_INNER, bw), lambda b, k: (b, 0, 0, k)),
        out_shape=jax.ShapeDtypeStruct((batch, 2, FFT_INNER, l1 * bw), BF16),
        compiler_params=_params("parallel", "parallel"),
        name="fourier_stage2",
    )(tabs["g2"], a.reshape(batch, 2, l1, FFT_INNER, bw))
    y4 = y.reshape(batch, 2, seq, bw)
    tm3 = min(seq, 1024)
    nt = seq // tm3
    tab = pl.BlockSpec((gdim, gdim), lambda b, i, g: (0, 0))
    lat = pl.pallas_call(
        _f3_kernel,
        grid=(batch, nt, FOURIER_GROUPS),
        in_specs=[pl.BlockSpec((None, None, tm3, gdim), lambda b, i, g: (b, 0, i, g)),
                  pl.BlockSpec((None, None, tm3, gdim), lambda b, i, g: (b, 1, i, g)), tab, tab],
        out_specs=pl.BlockSpec((tm3, gdim), lambda b, i, g: (b * nt + i, g)),
        out_shape=jax.ShapeDtypeStruct((batch * seq, bw), BF16),
        compiler_params=_params("parallel", "parallel", "parallel"),
        name="fourier_stage3",
    )(y4, y4, tabs["c3"], tabs["s3"])
    ctx_row = batch * seq // ctx_len
    tab2 = pl.BlockSpec((gdim, gdim), lambda b, g: (0, 0))
    ctx = pl.pallas_call(
        _fctx_kernel,
        grid=(batch, FOURIER_GROUPS),
        in_specs=[pl.BlockSpec((2 * ctx_len, ctx_len), lambda b, g: (0, 0)),
                  pl.BlockSpec((ctx_len, gdim), lambda b, g: (ctx_row + b, g)), tab2, tab2],
        out_specs=pl.BlockSpec((ctx_len, gdim), lambda b, g: (b, g)),
        out_shape=jax.ShapeDtypeStruct((batch * ctx_len, bw), BF16),
        compiler_params=_params("parallel", "parallel"),
        name="fourier_context",
    )(tabs["fc"], pf, tabs["c3c"], tabs["s3c"])
    return lat, ctx


def _pick(n, prefs):
    for t in prefs:
        if n % t == 0:
            return t
    return n


def kernel(x, c, ctx, c_ctx, norm_g, ada_down, ada_up, ada_b, ffn_w13, ffn_w2, w_in, conv_w, swa_q_norm,
           swa_k_norm, swa_sink, diff_q_norm, diff_k_norm, diff_lambda, diff_subln, w_branch, gate_up, w_out):
    batch, seq, d = x.shape
    ctx_len = ctx.shape[1]
    depth = norm_g.shape[0]
    bw = d // 4
    rank = gate_up.shape[1]
    d_ff = ffn_w2.shape[2]
    n_lat = batch * seq
    n_ctx = batch * ctx_len
    m = n_lat + n_ctx
    tm, tr = ROW_TILE, ELT_ROW_TILE
    swa_hd = bw // SWA_HEADS
    diff_hd = bw // (2 * DIFF_HEADS)
    assert swa_hd == LANES and 2 * diff_hd == LANES
    assert seq % tm == 0 and n_ctx % tm == 0 and seq % FFT_INNER == 0 and seq % GRID_W == 0
    assert seq & (seq - 1) == 0 and ctx_len & (ctx_len - 1) == 0 and n_lat % ctx_len == 0
    assert w_in.shape[2] == 8 * bw + 2 * SWA_KV_HEADS * swa_hd + rank

    def mod_row_for(layer):
        def mod_row(i, tile):
            who = jnp.minimum(i // (seq // tile), batch)
            return (layer * (batch + 1) + who) * N_MOD
        return mod_row

    def rope_row(i, tile):
        return jnp.where(i < n_lat // tile, i % (seq // tile), seq // tile)

    cv = jnp.concatenate([c, c_ctx[None, :], jnp.zeros((SUBLANES - batch - 1, d), F32)], axis=0)
    mods = _ada(cv, ada_down, ada_up, ada_b)[:, :batch + 1]
    mods = mods.reshape(depth * (batch + 1) * N_MOD, 1, d)
    gains = norm_g.reshape(depth * 3, 1, d)

    w13 = ffn_w13.astype(BF16)
    w2 = ffn_w2.astype(BF16)
    w_in_b = w_in.astype(BF16)
    w_branch_b = w_branch.astype(BF16)
    gate_up_b = gate_up.astype(BF16).reshape(depth, rank, -1)
    w_out_b = w_out.astype(BF16)

    cos_s, sin_s = _rope_tables(seq, swa_hd, tr)
    cos_d, sin_d = _rope_tables(seq, diff_hd, tr)
    tabs = _fourier_tables(seq, ctx_len, bw // FOURIER_GROUPS)

    tn_ff = _pick(d_ff, (512, 256, 128))
    tn_d = _pick(d, (512, 256, 128))
    tn_bw = _pick(bw, (512, 256, 128))
    kv_w = SWA_KV_HEADS * swa_hd
    sw_w = bw + 2 * kv_w
    tn_sw = _pick(math.gcd(sw_w, 4 * bw), (512, 256, 128))
    tn_df = _pick(math.gcd(3 * bw, 5 * bw + 2 * kv_w), (512, 256, 128))
    tn_z = _pick(math.gcd(rank, 8 * bw + 2 * kv_w), (512, 256, 128))
    ones_kv = jnp.ones((kv_w,), F32)
    swa_gains = jnp.concatenate([jnp.tile(swa_q_norm, (1, SWA_HEADS)), jnp.tile(swa_k_norm, (1, SWA_KV_HEADS)),
                                 jnp.tile(ones_kv[None], (depth, 1))], axis=1).reshape(depth, sw_w // kv_w, 1, kv_w)
    diff_gains = jnp.concatenate([jnp.tile(diff_q_norm, (1, 2 * DIFF_HEADS)), jnp.tile(diff_k_norm, (1, 2 * DIFF_HEADS)),
                                  jnp.ones((depth, bw), F32)], axis=1).reshape(depth, 3, 1, bw)
    subln = diff_subln.reshape(depth, 1, 2 * diff_hd)

    t = jnp.concatenate([x.reshape(n_lat, d), ctx.reshape(n_ctx, d)], axis=0)

    def ffn_half(t, layer, sub, k0, rows):
        mod_row = mod_row_for(layer)
        h = _normmod(t, gains, layer * 3 + 2 * sub, mods, mod_row, k0, k0 + 1, tr)
        act = _mm_swiglu(h, w13, (layer, sub), tn_ff, tm)
        return _mm_resid(act, w2, (layer, sub), t, mods, mod_row, k0 + 2, 0.5, tn_d, tm, rows)

    for layer in range(depth):
        last = layer == depth - 1
        rows = n_lat if last else m
        mod_row = mod_row_for(layer)
        lam_init = 0.8 - 0.6 * math.exp(-0.3 * layer)
        t = ffn_half(t, layer, 0, 0, m)
        h = _normmod(t, gains, layer * 3 + 1, mods, mod_row, 3, 4, tr)
        pf = _mm(h, w_in_b, (layer,), 0, bw, tn_bw, BF16, tm)
        pcv = _mm(h, w_in_b, (layer,), bw, 3 * bw, tn_bw, F32, tm)
        psw = _mm(h, w_in_b, (layer,), 4 * bw, sw_w, tn_sw, F32, tm)
        pdf = _mm(h, w_in_b, (layer,), 4 * bw + sw_w, 3 * bw, tn_df, F32, tm)
        pz = _mm(h, w_in_b, (layer,), 7 * bw + sw_w, rank, tn_z, BF16, tm)

        o_f = jnp.concatenate(_fourier(pf, tabs, batch, seq, ctx_len), axis=0)
        o_c = _conv(pcv, conv_w, layer, seq, ctx_len, n_lat, tr)
        qkv_s = _prep(psw, swa_gains[layer], cos_s, sin_s, kv_w, swa_hd, swa_hd // 4,
                      SWA_HEADS // SWA_KV_HEADS + 1, 0, 1.0, rope_row, tr)
        o_s = jnp.concatenate(_swa(qkv_s, swa_sink[layer], swa_hd, batch, seq, ctx_len), axis=0)
        qkv_d = _prep(pdf, diff_gains[layer], cos_d, sin_d, bw, diff_hd, diff_hd // 4,
                      2, 1, diff_hd ** -0.5, rope_row, tr)
        o_d = jnp.concatenate(_diff(qkv_d, diff_lambda, subln, layer, lam_init, batch, seq, ctx_len,
                                    min(seq, 256), min(seq, 256)), axis=0)

        merged = _merge([o_f, o_c, o_s, o_d], pz, w_branch_b, gate_up_b, layer, tn_d, tm, rows)
        t = _mm_resid(merged, w_out_b, (layer,), t, mods, mod_row, 5, 1.0, tn_d, tm, rows)
        t = ffn_half(t, layer, 1, 6, rows)
    return t.reshape(batch, seq, d)
```

```python
import functools
import math

import jax
import jax.numpy as jnp
from jax import lax
from jax.experimental import pallas as pl
from jax.experimental.pallas import tpu as pltpu

F32 = jnp.float32
BF16 = jnp.bfloat16

EPS = 1e-6
NEG_INF = -1e30
ROPE_BASE = 10000.0
GRID_W = 64
N_MOD = 9
FOURIER_GROUPS = 4
SWA_HEADS = 8
SWA_KV_HEADS = 2
SWA_BLOCK = 128
SWA_WINDOW = 128
DIFF_HEADS = 8
LANES = 128
SUBLANES = 8
FFT_INNER = 128

VMEM_LIMIT_BYTES = 56 * 1024 * 1024
ROW_TILE = 512
ELT_ROW_TILE = 256
DIFF_Q_TILE = 512
DIFF_K_TILE = 512
LOG2_E = 1.4426950408889634
TILE_TRIALS = ((512, 512, 512, 512), (1024, 1024, 1024, 1024), (640, 256, 256, 512), (256, 512, 512, 1024))

_HIGHEST = lax.Precision.HIGHEST
_NT_DIMS = (((1,), (1,)), ((), ()))


def _params(*sem):
    return pltpu.CompilerParams(dimension_semantics=sem, vmem_limit_bytes=VMEM_LIMIT_BYTES)


def _dot(a, b):
    return jnp.dot(a, b, preferred_element_type=F32)


def _ada_kernel(cv_ref, down_ref, up_ref, b_ref, o_ref, hid_ref):
    @pl.when(pl.program_id(1) == 0)
    def _():
        c = cv_ref[...]
        s = c * jax.nn.sigmoid(c)
        hid_ref[...] = jnp.dot(s, down_ref[...], precision=_HIGHEST, preferred_element_type=F32)

    o_ref[...] = jnp.dot(hid_ref[...], up_ref[...], precision=_HIGHEST,
                         preferred_element_type=F32) + b_ref[...]


def _ada(cv, down, up, b):
    depth, d, r = down.shape
    n = up.shape[2]
    tn = n // N_MOD
    return pl.pallas_call(
        _ada_kernel,
        grid=(depth, n // tn),
        in_specs=[pl.BlockSpec((SUBLANES, d), lambda l, j: (0, 0)),
                  pl.BlockSpec((None, d, r), lambda l, j: (l, 0, 0)),
                  pl.BlockSpec((None, r, tn), lambda l, j: (l, 0, j)),
                  pl.BlockSpec((None, 1, tn), lambda l, j: (l, 0, j))],
        out_specs=pl.BlockSpec((None, SUBLANES, tn), lambda l, j: (l, 0, j)),
        out_shape=jax.ShapeDtypeStruct((depth, SUBLANES, n), F32),
        scratch_shapes=[pltpu.VMEM((SUBLANES, r), F32)],
        compiler_params=_params("parallel", "arbitrary"),
        name="ada_modulation",
    )(cv, down, up, b.reshape(depth, 1, n))


def _normmod_kernel(t_ref, g_ref, sh_ref, sc_ref, o_ref):
    x = t_ref[...]
    ms = jnp.mean(x * x, axis=-1, keepdims=True)
    y = x * lax.rsqrt(ms + EPS) * g_ref[...]
    o_ref[...] = (y * (1.0 + sc_ref[...]) + sh_ref[...]).astype(o_ref.dtype)


def _normmod(t, gains, g_row, mods, mod_row, k_shift, k_scale, tr):
    m, d = t.shape
    vec = lambda idx: pl.BlockSpec((None, 1, d), idx)
    return pl.pallas_call(
        _normmod_kernel,
        grid=(m // tr,),
        in_specs=[pl.BlockSpec((tr, d), lambda i: (i, 0)),
                  vec(lambda i: (g_row, 0, 0)),
                  vec(lambda i: (mod_row(i, tr) + k_shift, 0, 0)),
                  vec(lambda i: (mod_row(i, tr) + k_scale, 0, 0))],
        out_specs=pl.BlockSpec((tr, d), lambda i: (i, 0)),
        out_shape=jax.ShapeDtypeStruct((m, d), BF16),
        compiler_params=_params("parallel"),
        name="norm_modulate",
    )(t, gains, mods, mods)


def _mm_kernel(x_ref, w_ref, o_ref):
    o_ref[...] = _dot(x_ref[...], w_ref[...]).astype(o_ref.dtype)


def _w_spec(widx, k, tn, col_block):
    lead = (None,) * len(widx)
    return pl.BlockSpec(lead + (k, tn), lambda i, j: widx + (0, col_block + j))


def _mm(x, w, widx, col_off, ncols, tn, out_dtype, tm):
    m, k = x.shape
    return pl.pallas_call(
        _mm_kernel,
        grid=(m // tm, ncols // tn),
        in_specs=[pl.BlockSpec((tm, k), lambda i, j: (i, 0)),
                  _w_spec(widx, k, tn, col_off // tn)],
        out_specs=pl.BlockSpec((tm, tn), lambda i, j: (i, j)),
        out_shape=jax.ShapeDtypeStruct((m, ncols), out_dtype),
        compiler_params=_params("parallel", "arbitrary"),
        name="matmul",
    )(x, w)


def _mm_swiglu_kernel(x_ref, wa_ref, wu_ref, o_ref):
    x = x_ref[...]
    a = _dot(x, wa_ref[...])
    u = _dot(x, wu_ref[...])
    o_ref[...] = (a * jax.nn.sigmoid(a) * u).astype(o_ref.dtype)


def _mm_swiglu(x, w13, widx, tn, tm):
    m, k = x.shape
    f = w13.shape[-1] // 2
    return pl.pallas_call(
        _mm_swiglu_kernel,
        grid=(m // tm, f // tn),
        in_specs=[pl.BlockSpec((tm, k), lambda i, j: (i, 0)),
                  _w_spec(widx, k, tn, 0),
                  _w_spec(widx, k, tn, f // tn)],
        out_specs=pl.BlockSpec((tm, tn), lambda i, j: (i, j)),
        out_shape=jax.ShapeDtypeStruct((m, f), BF16),
        compiler_params=_params("parallel", "arbitrary"),
        name="matmul_swiglu",
    )(x, w13, w13)


def _mm_resid_kernel(x_ref, w_ref, t_ref, g_ref, o_ref, *, coef):
    y = _dot(x_ref[...], w_ref[...])
    o_ref[...] = t_ref[...] + (coef * g_ref[...]) * y


def _mm_resid(x, w, widx, t, mods, mod_row, k_gate, coef, tn, tm, m_rows):
    k = x.shape[1]
    d = t.shape[1]
    return pl.pallas_call(
        functools.partial(_mm_resid_kernel, coef=coef),
        grid=(m_rows // tm, d // tn),
        in_specs=[pl.BlockSpec((tm, k), lambda i, j: (i, 0)),
                  _w_spec(widx, k, tn, 0),
                  pl.BlockSpec((tm, tn), lambda i, j: (i, j)),
                  pl.BlockSpec((None, 1, tn), lambda i, j: (mod_row(i, tm) + k_gate, 0, j))],
        out_specs=pl.BlockSpec((tm, tn), lambda i, j: (i, j)),
        out_shape=jax.ShapeDtypeStruct((m_rows, d), F32),
        compiler_params=_params("parallel", "arbitrary"),
        name="matmul_gated_residual",
    )(x, w, t, mods)


def _merge_kernel(o0_ref, o1_ref, o2_ref, o3_ref, z_ref, wb_ref, g0_ref, g1_ref, g2_ref, g3_ref, out_ref):
    z = z_ref[...]
    acc = None
    for i, (o_ref, g_ref) in enumerate(((o0_ref, g0_ref), (o1_ref, g1_ref), (o2_ref, g2_ref), (o3_ref, g3_ref))):
        term = jax.nn.sigmoid(_dot(z, g_ref[...])) * _dot(o_ref[...], wb_ref[i])
        acc = term if acc is None else acc + term
    out_ref[...] = acc.astype(out_ref.dtype)


def _merge(branches, z, w_branch, gate_up2, layer, tn, tm, m_rows):
    bw = branches[0].shape[1]
    r = z.shape[1]
    nbr, d = w_branch.shape[1], w_branch.shape[3]
    row = lambda width: pl.BlockSpec((tm, width), lambda i, j: (i, 0))
    gate = lambda b: pl.BlockSpec((None, r, tn), lambda i, j: (layer, 0, b * (d // tn) + j))
    return pl.pallas_call(
        _merge_kernel,
        grid=(m_rows // tm, d // tn),
        in_specs=[row(bw)] * nbr + [row(r),
                  pl.BlockSpec((None, nbr, bw, tn), lambda i, j: (layer, 0, 0, j))]
                 + [gate(b) for b in range(nbr)],
        out_specs=pl.BlockSpec((tm, tn), lambda i, j: (i, j)),
        out_shape=jax.ShapeDtypeStruct((m_rows, d), BF16),
        compiler_params=_params("parallel", "arbitrary"),
        name="gated_merge",
    )(*branches, z, w_branch, *([gate_up2] * nbr))


def _prep_kernel(x_ref, g_ref, cos_ref, sin_ref, o_ref, *, seg, half, n_rope, n_scaled, qscale):
    j = pl.program_id(1)
    width = x_ref.shape[1]

    @pl.when(j < n_rope)
    def _():
        scale = jnp.where(j < n_scaled, qscale, 1.0).astype(F32)
        cos = cos_ref[...]
        sin = sin_ref[...]
        ri = lax.broadcasted_iota(jnp.int32, (2 * LANES, LANES), 0) & (LANES - 1)
        ci = lax.broadcasted_iota(jnp.int32, (2 * LANES, LANES), 1)
        seg_ind = jnp.where((ri & -seg) == (ci & -seg), 1.0, 0.0).astype(BF16)
        for c in range(width // LANES):
            sl = slice(c * LANES, (c + 1) * LANES)
            x = x_ref[:, sl]
            lane = lax.broadcasted_iota(jnp.int32, x.shape, 1)
            ss = x * x
            ss_hi = ss.astype(BF16)
            ss_lo = (ss - ss_hi.astype(F32)).astype(BF16)
            ms = _dot(jnp.concatenate([ss_hi, ss_lo], axis=1), seg_ind) * (1.0 / seg)
            xn = x * lax.rsqrt(ms + EPS) * g_ref[:, sl]
            lower = pltpu.roll(xn, half, 1)
            upper = pltpu.roll(xn, LANES - half, 1)
            partner = jnp.where((lane & half) == 0, upper, lower)
            y = xn * cos + partner * sin
            o_ref[:, sl] = (y * scale).astype(o_ref.dtype)

    @pl.when(j >= n_rope)
    def _():
        o_ref[...] = x_ref[...].astype(o_ref.dtype)


def _prep(p, n, gains, cos_t, sin_t, width, seg, half, n_rope, n_scaled, qscale, rope_row, tr):
    m = p.shape[0]
    return pl.pallas_call(
        functools.partial(_prep_kernel, seg=seg, half=half, n_rope=n_rope, n_scaled=n_scaled, qscale=qscale),
        grid=(m // tr, n // width),
        in_specs=[pl.BlockSpec((tr, width), lambda i, j: (i, j)),
                  pl.BlockSpec((None, 1, width), lambda i, j: (j, 0, 0)),
                  pl.BlockSpec((tr, LANES), lambda i, j: (rope_row(i, tr), 0)),
                  pl.BlockSpec((tr, LANES), lambda i, j: (rope_row(i, tr), 0))],
        out_specs=pl.BlockSpec((tr, width), lambda i, j: (i, j)),
        out_shape=jax.ShapeDtypeStruct((m, n), BF16),
        compiler_params=_params("parallel", "arbitrary"),
        name="qk_norm_rope",
    )(p, gains, cos_t, sin_t)


def _rope_tables(seq, head_dim, ident_rows):
    rows = seq // GRID_W
    row = jnp.repeat(jnp.arange(rows, dtype=jnp.int32), GRID_W)
    col = jnp.tile(jnp.arange(GRID_W, dtype=jnp.int32), rows)
    axis_dim = head_dim // 2
    inv_freq = ROPE_BASE ** (-jnp.arange(0, axis_dim, 2, dtype=F32) / axis_dim)
    ar = row.astype(F32)[:, None] * inv_freq
    ac = col.astype(F32)[:, None] * inv_freq
    cr, sr, cc, sc = jnp.cos(ar), jnp.sin(ar), jnp.cos(ac), jnp.sin(ac)
    cos_t = jnp.concatenate([cr, cr, cc, cc], axis=-1)
    sin_t = jnp.concatenate([-sr, sr, -sc, sc], axis=-1)
    reps = LANES // head_dim
    cos_t = jnp.tile(cos_t, (1, reps))
    sin_t = jnp.tile(sin_t, (1, reps))
    cos_t = jnp.concatenate([cos_t, jnp.ones((ident_rows, LANES), F32)], axis=0)
    sin_t = jnp.concatenate([sin_t, jnp.zeros((ident_rows, LANES), F32)], axis=0)
    return cos_t, sin_t


def _conv_kernel(gb_ref, gc_ref, v_ref, gcp_ref, vp_ref, gcn_ref, vn_ref, w_ref, o_ref, *, seq, ctx_len, n_lat):
    tr = gb_ref.shape[0]
    i = pl.program_id(0)
    u = gc_ref[...] * v_ref[...]
    u_before = (gcp_ref[...] * vp_ref[...])[SUBLANES - 1:SUBLANES, :]
    u_after = (gcn_ref[...] * vn_ref[...])[0:1, :]
    r = lax.broadcasted_iota(jnp.int32, (tr, 1), 0)
    gr = i * tr + r
    is_lat = gr < n_lat
    pos = jnp.where(is_lat, gr & (seq - 1), (gr - n_lat) & (ctx_len - 1))
    last = jnp.where(is_lat, seq - 1, ctx_len - 1)
    u_m = jnp.where(r == 0, u_before, pltpu.roll(u, 1, 0))
    u_m = jnp.where(pos == 0, 0.0, u_m)
    u_p = jnp.where(r == tr - 1, u_after, pltpu.roll(u, tr - 1, 0))
    u_p = jnp.where(pos == last, 0.0, u_p)
    y = u_m * w_ref[0:1, :] + u * w_ref[1:2, :] + u_p * w_ref[2:3, :]
    o_ref[...] = (gb_ref[...] * y).astype(o_ref.dtype)


def _conv(pcv, conv_w, layer, seq, ctx_len, n_lat, tr):
    m = pcv.shape[0]
    bw = pcv.shape[1] // 3
    halo = tr // SUBLANES
    n_halo = m // SUBLANES
    main = lambda c: pl.BlockSpec((tr, bw), lambda i: (i, c))
    before = lambda c: pl.BlockSpec((SUBLANES, bw), lambda i: (jnp.maximum(i * halo - 1, 0), c))
    after = lambda c: pl.BlockSpec((SUBLANES, bw), lambda i: (jnp.minimum((i + 1) * halo, n_halo - 1), c))
    return pl.pallas_call(
        functools.partial(_conv_kernel, seq=seq, ctx_len=ctx_len, n_lat=n_lat),
        grid=(m // tr,),
        in_specs=[main(0), main(1), main(2), before(1), before(2), after(1), after(2),
                  pl.BlockSpec((None, 3, bw), lambda i: (layer, 0, 0))],
        out_specs=pl.BlockSpec((tr, bw), lambda i: (i, 0)),
        out_shape=jax.ShapeDtypeStruct((m, bw), BF16),
        compiler_params=_params("parallel"),
        name="short_conv",
    )(pcv, pcv, pcv, pcv, pcv, pcv, pcv, conv_w)


def _sink_column(sink_ref, h, groups, blk):
    r = lax.broadcasted_iota(jnp.int32, (groups * blk, 1), 0)
    col = jnp.zeros((groups * blk, 1), F32)
    for g in range(groups):
        col = jnp.where((r >= g * blk) & (r < (g + 1) * blk), sink_ref[h * groups + g], col)
    return col


def _stack_heads(q, groups, hd):
    return jnp.concatenate([q[:, g * hd:(g + 1) * hd] for g in range(groups)], axis=0)


def _swa_lat_kernel(sink_ref, q_ref, kp_ref, kc_ref, kn_ref, vp_ref, vc_ref, vn_ref, kx_ref, vx_ref, o_ref,
                    *, seq, groups, hd, scale):
    n = pl.program_id(1)
    h = pl.program_id(2)
    blk = q_ref.shape[0]
    q = _stack_heads(q_ref[...], groups, hd)
    kw = jnp.concatenate([kp_ref[...], kc_ref[...], kn_ref[...]], axis=0)
    vw = jnp.concatenate([vp_ref[...], vc_ref[...], vn_ref[...]], axis=0)
    s_loc = lax.dot_general(q, kw, _NT_DIMS, preferred_element_type=F32) * scale
    s_ctx = lax.dot_general(q, kx_ref[...], _NT_DIMS, preferred_element_type=F32) * scale
    ri = lax.broadcasted_iota(jnp.int32, s_loc.shape, 0)
    ci = lax.broadcasted_iota(jnp.int32, s_loc.shape, 1)
    rel = ci - blk - (ri & (blk - 1))
    kpos = (n - 1) * blk + ci
    valid = (jnp.abs(rel) <= SWA_WINDOW) & (kpos >= 0) & (kpos < seq)
    s_loc = jnp.where(valid, s_loc, NEG_INF)
    sink = _sink_column(sink_ref, h, groups, blk)
    m = jnp.maximum(jnp.maximum(jnp.max(s_ctx, axis=-1, keepdims=True),
                                jnp.max(s_loc, axis=-1, keepdims=True)), sink)
    e_ctx = jnp.exp(s_ctx - m)
    e_loc = jnp.exp(s_loc - m)
    den = (jnp.sum(e_ctx, axis=-1, keepdims=True) + jnp.sum(e_loc, axis=-1, keepdims=True)
           + jnp.exp(sink - m))
    o = (_dot(e_ctx.astype(BF16), vx_ref[...]) + _dot(e_loc.astype(BF16), vw)) / den
    for g in range(groups):
        o_ref[:, g * hd:(g + 1) * hd] = o[g * blk:(g + 1) * blk].astype(o_ref.dtype)


def _swa_ctx_kernel(sink_ref, q_ref, kx_ref, vx_ref, o_ref, *, groups, hd, scale):
    h = pl.program_id(1)
    blk = q_ref.shape[0]
    q = _stack_heads(q_ref[...], groups, hd)
    s = lax.dot_general(q, kx_ref[...], _NT_DIMS, preferred_element_type=F32) * scale
    sink = _sink_column(sink_ref, h, groups, blk)
    m = jnp.maximum(jnp.max(s, axis=-1, keepdims=True), sink)
    e = jnp.exp(s - m)
    den = jnp.sum(e, axis=-1, keepdims=True) + jnp.exp(sink - m)
    o = _dot(e.astype(BF16), vx_ref[...]) / den
    for g in range(groups):
        o_ref[:, g * hd:(g + 1) * hd] = o[g * blk:(g + 1) * blk].astype(o_ref.dtype)


def _swa(qkv, sink, hd, batch, seq, ctx_len):
    bw = SWA_HEADS * hd
    groups = SWA_HEADS // SWA_KV_HEADS
    gw = groups * hd
    scale = hd ** -0.5
    blk = SWA_BLOCK
    nb = seq // blk
    kcol = bw // hd
    vcol = kcol + SWA_KV_HEADS
    ctx_row = batch * seq // ctx_len
    smem = pl.BlockSpec(memory_space=pltpu.SMEM)

    def kv(col, shift):
        def idx(b, n, h):
            return (b * nb + jnp.clip(n + shift, 0, nb - 1), col + h)
        return pl.BlockSpec((blk, hd), idx)

    ctx_kv = lambda col: pl.BlockSpec((ctx_len, hd), lambda b, n, h: (ctx_row + b, col + h))
    lat = pl.pallas_call(
        functools.partial(_swa_lat_kernel, seq=seq, groups=groups, hd=hd, scale=scale),
        grid=(batch, nb, SWA_KV_HEADS),
        in_specs=[smem, pl.BlockSpec((blk, gw), lambda b, n, h: (b * nb + n, h)),
                  kv(kcol, -1), kv(kcol, 0), kv(kcol, 1), kv(vcol, -1), kv(vcol, 0), kv(vcol, 1),
                  ctx_kv(kcol), ctx_kv(vcol)],
        out_specs=pl.BlockSpec((blk, gw), lambda b, n, h: (b * nb + n, h)),
        out_shape=jax.ShapeDtypeStruct((batch * seq, bw), BF16),
        compiler_params=_params("parallel", "parallel", "parallel"),
        name="swa_latent",
    )(sink, qkv, qkv, qkv, qkv, qkv, qkv, qkv, qkv, qkv)
    ctx_kv2 = lambda col: pl.BlockSpec((ctx_len, hd), lambda b, h: (ctx_row + b, col + h))
    ctx = pl.pallas_call(
        functools.partial(_swa_ctx_kernel, groups=groups, hd=hd, scale=scale),
        grid=(batch, SWA_KV_HEADS),
        in_specs=[smem, pl.BlockSpec((ctx_len, gw), lambda b, h: (ctx_row + b, h)),
                  ctx_kv2(kcol), ctx_kv2(vcol)],
        out_specs=pl.BlockSpec((ctx_len, gw), lambda b, h: (b, h)),
        out_shape=jax.ShapeDtypeStruct((batch * ctx_len, bw), BF16),
        compiler_params=_params("parallel", "parallel"),
        name="swa_context",
    )(sink, qkv, qkv, qkv)
    return lat, ctx


def _vt_kernel(v_ref, o_ref):
    heads, vrows, tr = o_ref.shape
    hd2 = v_ref.shape[1] // heads
    for h in range(heads):
        o_ref[h, :hd2, :] = v_ref[:, h * hd2:(h + 1) * hd2].T.astype(o_ref.dtype)
        o_ref[h, hd2:, :] = jnp.ones((vrows - hd2, tr), o_ref.dtype)


def _vt(p, col_block, heads, tr):
    m = p.shape[0]
    hd2 = LANES
    vrows = hd2 + 2 * SUBLANES
    return pl.pallas_call(
        _vt_kernel,
        grid=(m // tr,),
        in_specs=[pl.BlockSpec((tr, heads * hd2), lambda i: (i, col_block))],
        out_specs=pl.BlockSpec((heads, vrows, tr), lambda i: (0, 0, i)),
        out_shape=jax.ShapeDtypeStruct((heads, vrows, m), BF16),
        compiler_params=_params("parallel"),
        name="v_transpose",
    )(p)


def _diff_kernel(dl_ref, g_ref, q_ref, kx_ref, vx_ref, *rest, lam_init, tk, with_latent):
    if with_latent:
        k_ref, v_ref, o_ref, m_ref, acc_ref, s0_ref, s1_ref = rest
    else:
        o_ref, m_ref, acc_ref = rest
    tq, hd2 = q_ref.shape
    hd = hd2 // 2
    qf = q_ref[...].astype(F32)
    lane = lax.broadcasted_iota(jnp.int32, qf.shape, 1)
    qcat = jnp.concatenate([jnp.where(lane < hd, qf, 0.0), jnp.where(lane < hd, 0.0, qf)], axis=0).astype(BF16)

    def scores(k):
        return lax.dot_general(k, qcat, _NT_DIMS, preferred_element_type=F32)

    def absorb(s, vt):
        m_old = m_ref[...]
        m_new = jnp.maximum(m_old, jnp.max(s, axis=0, keepdims=True))
        alpha = jnp.exp2(m_old - m_new)
        p = jnp.exp2(s - m_new).astype(BF16)
        acc_ref[...] = alpha * acc_ref[...] + _dot(vt, p)
        m_ref[...] = m_new

    m_ref[...] = jnp.full(m_ref.shape, NEG_INF, F32)
    acc_ref[...] = jnp.zeros(acc_ref.shape, F32)
    if with_latent:
        def chunk_rows(c):
            off = c * tk
            return pl.ds(off if isinstance(off, int) else pl.multiple_of(off, tk), tk)

        def keys(c):
            return k_ref[chunk_rows(c), :]

        def vals(c):
            return v_ref[:, chunk_rows(c)]

        n_pairs = k_ref.shape[0] // (2 * tk)
        s0_ref[...] = scores(keys(0))
        absorb(scores(kx_ref[...]), vx_ref[...])

        def body(c2, carry):
            c = 2 * c2
            s1_ref[...] = scores(keys(c + 1))
            absorb(s0_ref[...], vals(c))
            s0_ref[...] = scores(keys(c + 2))
            absorb(s1_ref[...], vals(c + 1))
            return carry
        lax.fori_loop(0, n_pairs - 1, body, 0)
        c = 2 * (n_pairs - 1)
        s1_ref[...] = scores(keys(c + 1))
        absorb(s0_ref[...], vals(c))
        absorb(s1_ref[...], vals(c + 1))
    else:
        absorb(scores(kx_ref[...]), vx_ref[...])
    acc = acc_ref[...]
    on = acc[:hd2] / acc[hd2:hd2 + 1]
    dl = dl_ref[...]
    lam = (jnp.exp(jnp.sum(dl[0:1] * dl[1:2], axis=-1, keepdims=True))
           - jnp.exp(jnp.sum(dl[2:3] * dl[3:4], axis=-1, keepdims=True)) + lam_init)
    o = (on[:, :tq] - lam * on[:, tq:]).T
    y = o * lax.rsqrt(jnp.mean(o * o, axis=-1, keepdims=True) + EPS) * g_ref[...]
    o_ref[...] = (y * (1.0 - lam_init)).astype(o_ref.dtype)


def _diff(qk, vt, diff_lambda, subln, layer, lam_init, batch, seq, ctx_len, tq, tk):
    bw = qk.shape[1] // 2
    hd2 = bw // DIFF_HEADS
    vrows = vt.shape[1]
    kcol = DIFF_HEADS
    ctx_row = batch * seq // ctx_len
    nq = seq // tq
    dl_spec3 = pl.BlockSpec((None,) + diff_lambda.shape[1:], lambda b, h, i: (layer, 0, 0))
    g_spec3 = pl.BlockSpec((None, 1, hd2), lambda b, h, i: (layer, 0, 0))
    lat = pl.pallas_call(
        functools.partial(_diff_kernel, lam_init=lam_init, tk=tk, with_latent=True),
        grid=(batch, DIFF_HEADS, nq),
        in_specs=[dl_spec3, g_spec3,
                  pl.BlockSpec((tq, hd2), lambda b, h, i: (b * nq + i, h)),
                  pl.BlockSpec((ctx_len, hd2), lambda b, h, i: (ctx_row + b, kcol + h)),
                  pl.BlockSpec((None, vrows, ctx_len), lambda b, h, i: (h, 0, ctx_row + b)),
                  pl.BlockSpec((seq, hd2), lambda b, h, i: (b, kcol + h)),
                  pl.BlockSpec((None, vrows, seq), lambda b, h, i: (h, 0, b))],
        out_specs=pl.BlockSpec((tq, hd2), lambda b, h, i: (b * nq + i, h)),
        out_shape=jax.ShapeDtypeStruct((batch * seq, bw), BF16),
        scratch_shapes=[pltpu.VMEM((1, 2 * tq), F32), pltpu.VMEM((vrows, 2 * tq), F32),
                        pltpu.VMEM((tk, 2 * tq), F32), pltpu.VMEM((tk, 2 * tq), F32)],
        compiler_params=_params("parallel", "parallel", "arbitrary"),
        name="diff_latent",
    )(diff_lambda, subln, qk, qk, vt, qk, vt)
    dl_spec2 = pl.BlockSpec((None,) + diff_lambda.shape[1:], lambda b, h: (layer, 0, 0))
    g_spec2 = pl.BlockSpec((None, 1, hd2), lambda b, h: (layer, 0, 0))
    ctx = pl.pallas_call(
        functools.partial(_diff_kernel, lam_init=lam_init, tk=tk, with_latent=False),
        grid=(batch, DIFF_HEADS),
        in_specs=[dl_spec2, g_spec2,
                  pl.BlockSpec((ctx_len, hd2), lambda b, h: (ctx_row + b, h)),
                  pl.BlockSpec((ctx_len, hd2), lambda b, h: (ctx_row + b, kcol + h)),
                  pl.BlockSpec((None, vrows, ctx_len), lambda b, h: (h, 0, ctx_row + b))],
        out_specs=pl.BlockSpec((ctx_len, hd2), lambda b, h: (b, h)),
        out_shape=jax.ShapeDtypeStruct((batch * ctx_len, bw), BF16),
        scratch_shapes=[pltpu.VMEM((1, 2 * ctx_len), F32), pltpu.VMEM((vrows, 2 * ctx_len), F32)],
        compiler_params=_params("parallel", "parallel"),
        name="diff_context",
    )(diff_lambda, subln, qk, qk, vt)
    return lat, ctx


def _f1_kernel(f_ref, u_ref, a_ref):
    l1 = u_ref.shape[0]
    r = _dot(f_ref[...], u_ref[...])
    a_ref[0] = r[:l1].astype(a_ref.dtype)
    a_ref[1] = r[l1:].astype(a_ref.dtype)


def _f2_kernel(g_ref, a_ref, y_ref):
    n = a_ref.shape[1]
    r = _dot(g_ref[:, :n], a_ref[0]) + _dot(g_ref[:, n:], a_ref[1])
    y_ref[0] = r[:n].astype(y_ref.dtype)
    y_ref[1] = r[n:].astype(y_ref.dtype)


def _f3_kernel(yr_ref, yi_ref, c_ref, s_ref, o_ref):
    o_ref[...] = (_dot(yr_ref[...], c_ref[...]) + _dot(yi_ref[...], s_ref[...])).astype(o_ref.dtype)


def _fctx_kernel(f_ref, u_ref, c_ref, s_ref, o_ref):
    n = u_ref.shape[0]
    w = _dot(f_ref[...], u_ref[...])
    wr = w[:n].astype(BF16)
    wi = w[n:].astype(BF16)
    o_ref[...] = (_dot(wr, c_ref[...]) + _dot(wi, s_ref[...])).astype(o_ref.dtype)


def _angle(prod, n):
    return (2.0 * math.pi / n) * (prod % n).astype(F32)


def _dft_pair(n):
    k = jnp.arange(n, dtype=jnp.int32)
    ang = _angle(k[:, None] * k[None, :], n)
    return jnp.concatenate([jnp.cos(ang), -jnp.sin(ang)], axis=0)


def _fourier_tables(seq, ctx_len, gdim):
    l1 = seq // FFT_INNER
    f1 = _dft_pair(l1).astype(BF16)
    k1 = jnp.arange(l1, dtype=jnp.int32)[:, None, None]
    k2 = jnp.arange(FFT_INNER, dtype=jnp.int32)[None, :, None]
    t2 = jnp.arange(FFT_INNER, dtype=jnp.int32)[None, None, :]
    ang = _angle(t2 * (k1 + l1 * k2), seq)
    c, s = jnp.cos(ang), jnp.sin(ang)
    g2 = jnp.concatenate([jnp.concatenate([c, s], axis=2), jnp.concatenate([-s, c], axis=2)], axis=1).astype(BF16)
    kc = jnp.arange(gdim, dtype=jnp.int32)
    angc = _angle(kc[:, None] * kc[None, :], gdim)
    cc, sc = jnp.cos(angc), jnp.sin(angc)
    nl = 1.0 / math.sqrt(seq * gdim)
    nc = 1.0 / math.sqrt(ctx_len * gdim)
    return dict(f1=f1, g2=g2, c3=(cc * nl).astype(BF16), s3=(sc * nl).astype(BF16),
                fc=_dft_pair(ctx_len).astype(BF16), c3c=(cc * nc).astype(BF16), s3c=(sc * nc).astype(BF16))


def _fourier(pf, tabs, batch, seq, ctx_len):
    m, bw = pf.shape
    gdim = bw // FOURIER_GROUPS
    l1 = seq // FFT_INNER
    flat = FFT_INNER * bw
    cc1 = min(flat, 8192)
    a = pl.pallas_call(
        _f1_kernel,
        grid=(batch, flat // cc1),
        in_specs=[pl.BlockSpec((2 * l1, l1), lambda b, j: (0, 0)),
                  pl.BlockSpec((l1, cc1), lambda b, j: (b, j))],
        out_specs=pl.BlockSpec((None, 2, l1, cc1), lambda b, j: (b, 0, 0, j)),
        out_shape=jax.ShapeDtypeStruct((batch, 2, l1, flat), BF16),
        compiler_params=_params("parallel", "parallel"),
        name="fourier_stage1",
    )(tabs["f1"], pf.reshape(m // FFT_INNER, flat))
    y = pl.pallas_call(
        _f2_kernel,
        grid=(batch, l1),
        in_specs=[pl.BlockSpec((None, 2 * FFT_INNER, 2 * FFT_INNER), lambda b, k: (k, 0, 0)),
                  pl.BlockSpec((None, 2, None, FFT_INNER, bw), lambda b, k: (b, 0, k, 0, 0))],
        out_specs=pl.BlockSpec((None, 2, FFT_INNER, bw), lambda b, k: (b, 0, 0, k)),
        out_shape=jax.ShapeDtypeStruct((batch, 2, FFT_INNER, l1 * bw), BF16),
        compiler_params=_params("parallel", "parallel"),
        name="fourier_stage2",
    )(tabs["g2"], a.reshape(batch, 2, l1, FFT_INNER, bw))
    y4 = y.reshape(batch, 2, seq, bw)
    tm3 = min(seq, 1024)
    nt = seq // tm3
    tab = pl.BlockSpec((gdim, gdim), lambda b, i, g: (0, 0))
    lat = pl.pallas_call(
        _f3_kernel,
        grid=(batch, nt, FOURIER_GROUPS),
        in_specs=[pl.BlockSpec((None, None, tm3, gdim), lambda b, i, g: (b, 0, i, g)),
                  pl.BlockSpec((None, None, tm3, gdim), lambda b, i, g: (b, 1, i, g)), tab, tab],
        out_specs=pl.BlockSpec((tm3, gdim), lambda b, i, g: (b * nt + i, g)),
        out_shape=jax.ShapeDtypeStruct((batch * seq, bw), BF16),
        compiler_params=_params("parallel", "parallel", "parallel"),
        name="fourier_stage3",
    )(y4, y4, tabs["c3"], tabs["s3"])
    ctx_row = batch * seq // ctx_len
    tab2 = pl.BlockSpec((gdim, gdim), lambda b, g: (0, 0))
    ctx = pl.pallas_call(
        _fctx_kernel,
        grid=(batch, FOURIER_GROUPS),
        in_specs=[pl.BlockSpec((2 * ctx_len, ctx_len), lambda b, g: (0, 0)),
                  pl.BlockSpec((ctx_len, gdim), lambda b, g: (ctx_row + b, g)), tab2, tab2],
        out_specs=pl.BlockSpec((ctx_len, gdim), lambda b, g: (b, g)),
        out_shape=jax.ShapeDtypeStruct((batch * ctx_len, bw), BF16),
        compiler_params=_params("parallel", "parallel"),
        name="fourier_context",
    )(tabs["fc"], pf, tabs["c3c"], tabs["s3c"])
    return lat, ctx


def _pick(n, prefs):
    for t in prefs:
        if n % t == 0:
            return t
    return n


def kernel(x, c, ctx, c_ctx, norm_g, ada_down, ada_up, ada_b, ffn_w13, ffn_w2, w_in, conv_w, swa_q_norm,
           swa_k_norm, swa_sink, diff_q_norm, diff_k_norm, diff_lambda, diff_subln, w_branch, gate_up, w_out):
    batch, seq, d = x.shape
    ctx_len = ctx.shape[1]
    depth = norm_g.shape[0]
    bw = d // 4
    rank = gate_up.shape[1]
    d_ff = ffn_w2.shape[2]
    n_lat = batch * seq
    n_ctx = batch * ctx_len
    m = n_lat + n_ctx
    tm, tr = ROW_TILE, ELT_ROW_TILE
    swa_hd = bw // SWA_HEADS
    diff_hd = bw // (2 * DIFF_HEADS)
    assert swa_hd == LANES and 2 * diff_hd == LANES
    assert seq % tm == 0 and n_ctx % tm == 0 and seq % FFT_INNER == 0 and seq % GRID_W == 0
    assert seq & (seq - 1) == 0 and ctx_len & (ctx_len - 1) == 0 and n_lat % ctx_len == 0
    assert w_in.shape[2] == 8 * bw + 2 * SWA_KV_HEADS * swa_hd + rank

    def mod_row_for(layer):
        def mod_row(i, tile):
            who = jnp.minimum(i // (seq // tile), batch)
            return (layer * (batch + 1) + who) * N_MOD
        return mod_row

    def rope_row(i, tile):
        return jnp.where(i < n_lat // tile, i % (seq // tile), seq // tile)

    cv = jnp.concatenate([c, c_ctx[None, :], jnp.zeros((SUBLANES - batch - 1, d), F32)], axis=0)
    mods = _ada(cv, ada_down, ada_up, ada_b)[:, :batch + 1]
    mods = mods.reshape(depth * (batch + 1) * N_MOD, 1, d)
    gains = norm_g.reshape(depth * 3, 1, d)

    w13 = ffn_w13.astype(BF16)
    w2 = ffn_w2.astype(BF16)
    w_in_b = w_in.astype(BF16)
    w_branch_b = w_branch.astype(BF16)
    gate_up_b = gate_up.astype(BF16).reshape(depth, rank, -1)
    w_out_b = w_out.astype(BF16)

    cos_s, sin_s = _rope_tables(seq, swa_hd, tr)
    cos_d, sin_d = _rope_tables(seq, diff_hd, tr)
    tabs = _fourier_tables(seq, ctx_len, bw // FOURIER_GROUPS)

    tn_ff = _pick(d_ff, (512, 256, 128))
    tn_d = _pick(d, (512, 256, 128))
    tn_bw = _pick(bw, (512, 256, 128))
    kv_w = SWA_KV_HEADS * swa_hd
    sw_w = bw + 2 * kv_w
    tn_sw = _pick(math.gcd(sw_w, 4 * bw), (512, 256, 128))
    tn_df = _pick(math.gcd(3 * bw, 5 * bw + 2 * kv_w), (512, 256, 128))
    tn_z = _pick(math.gcd(rank, 8 * bw + 2 * kv_w), (512, 256, 128))
    ones_kv = jnp.ones((kv_w,), F32)
    swa_gains = jnp.concatenate([jnp.tile(swa_q_norm, (1, SWA_HEADS)), jnp.tile(swa_k_norm, (1, SWA_KV_HEADS)),
                                 jnp.tile(ones_kv[None], (depth, 1))], axis=1).reshape(depth, sw_w // kv_w, 1, kv_w)
    diff_gains = jnp.concatenate([jnp.tile(diff_q_norm, (1, 2 * DIFF_HEADS)),
                                  jnp.tile(diff_k_norm, (1, 2 * DIFF_HEADS))], axis=1).reshape(depth, 2, 1, bw)
    subln = diff_subln.reshape(depth, 1, 2 * diff_hd)

    t = jnp.concatenate([x.reshape(n_lat, d), ctx.reshape(n_ctx, d)], axis=0)

    def ffn_half(t, layer, sub, k0, rows):
        mod_row = mod_row_for(layer)
        tn_ff, tn_res = trial_tiles(layer)[:2]
        h = _normmod(t, gains, layer * 3 + 2 * sub, mods, mod_row, k0, k0 + 1, tr)
        act = _mm_swiglu(h, w13, (layer, sub), tn_ff, tm)
        return _mm_resid(act, w2, (layer, sub), t, mods, mod_row, k0 + 2, 0.5, tn_res, tm, rows)

    def trial_tiles(layer):
        ff, res, mrg, win = TILE_TRIALS[layer % len(TILE_TRIALS)]
        return (_pick(d_ff, (ff, tn_ff)), _pick(d, (res, tn_d)), _pick(d, (mrg, tn_d)), _pick(bw, (win, tn_bw)))

    for layer in range(depth):
        last = layer == depth - 1
        _, tn_res, tn_mrg, tn_win = trial_tiles(layer)
        rows = n_lat if last else m
        mod_row = mod_row_for(layer)
        lam_init = 0.8 - 0.6 * math.exp(-0.3 * layer)
        t = ffn_half(t, layer, 0, 0, m)
        h = _normmod(t, gains, layer * 3 + 1, mods, mod_row, 3, 4, tr)
        pf = _mm(h, w_in_b, (layer,), 0, bw, tn_win, BF16, tm)
        pcv = _mm(h, w_in_b, (layer,), bw, 3 * bw, tn_win, F32, tm)
        psw = _mm(h, w_in_b, (layer,), 4 * bw, sw_w, tn_sw, F32, tm)
        pdf = _mm(h, w_in_b, (layer,), 4 * bw + sw_w, 3 * bw, tn_df, F32, tm)
        pz = _mm(h, w_in_b, (layer,), 7 * bw + sw_w, rank, tn_z, BF16, tm)

        o_f = jnp.concatenate(_fourier(pf, tabs, batch, seq, ctx_len), axis=0)
        o_c = _conv(pcv, conv_w, layer, seq, ctx_len, n_lat, tr)
        qkv_s = _prep(psw, sw_w, swa_gains[layer], cos_s, sin_s, kv_w, swa_hd, swa_hd // 4,
                      SWA_HEADS // SWA_KV_HEADS + 1, 0, 1.0, rope_row, tr)
        o_s = jnp.concatenate(_swa(qkv_s, swa_sink[layer], swa_hd, batch, seq, ctx_len), axis=0)
        qk_d = _prep(pdf, 2 * bw, diff_gains[layer], cos_d, sin_d, bw, diff_hd, diff_hd // 4,
                     2, 1, diff_hd ** -0.5 * LOG2_E, rope_row, tr)
        vt_d = _vt(pdf, 2, DIFF_HEADS, tr)
        o_d = jnp.concatenate(_diff(qk_d, vt_d, diff_lambda, subln, layer, lam_init, batch, seq, ctx_len,
                                    min(seq, DIFF_Q_TILE), min(seq, DIFF_K_TILE)), axis=0)

        merged = _merge([o_f, o_c, o_s, o_d], pz, w_branch_b, gate_up_b, layer, tn_mrg, tm, rows)
        t = _mm_resid(merged, w_out_b, (layer,), t, mods, mod_row, 5, 1.0, tn_res, tm, rows)
        t = ffn_half(t, layer, 1, 6, rows)
    return t.reshape(batch, seq, d)
```

```python
import functools
import math

import jax
import jax.numpy as jnp
from jax import lax
from jax.experimental import pallas as pl
from jax.experimental.pallas import tpu as pltpu

F32 = jnp.float32
BF16 = jnp.bfloat16

EPS = 1e-6
NEG_INF = -1e30
ROPE_BASE = 10000.0
GRID_W = 64
N_MOD = 9
FOURIER_GROUPS = 4
SWA_HEADS = 8
SWA_KV_HEADS = 2
SWA_BLOCK = 128
SWA_WINDOW = 128
DIFF_HEADS = 8
LANES = 128
SUBLANES = 8
FFT_INNER = 128

VMEM_LIMIT_BYTES = 56 * 1024 * 1024
ROW_TILE = 512
ELT_ROW_TILE = 256
DIFF_TRIALS = ((512, 512, F32), (512, 512, BF16), (512, 1024, F32), (1024, 512, F32))
LOG2_E = 1.4426950408889634
COL_TILES = (1024, 512, 256, 128)
SWA_Q_BLOCKS = 4

_HIGHEST = lax.Precision.HIGHEST
_NT_DIMS = (((1,), (1,)), ((), ()))


def _params(*sem):
    return pltpu.CompilerParams(dimension_semantics=sem, vmem_limit_bytes=VMEM_LIMIT_BYTES)


def _dot(a, b):
    return jnp.dot(a, b, preferred_element_type=F32)


def _ada_kernel(cv_ref, down_ref, up_ref, b_ref, o_ref, hid_ref):
    @pl.when(pl.program_id(1) == 0)
    def _():
        c = cv_ref[...]
        s = c * jax.nn.sigmoid(c)
        hid_ref[...] = jnp.dot(s, down_ref[...], precision=_HIGHEST, preferred_element_type=F32)

    o_ref[...] = jnp.dot(hid_ref[...], up_ref[...], precision=_HIGHEST,
                         preferred_element_type=F32) + b_ref[...]


def _ada(cv, down, up, b):
    depth, d, r = down.shape
    n = up.shape[2]
    tn = n // N_MOD
    return pl.pallas_call(
        _ada_kernel,
        grid=(depth, n // tn),
        in_specs=[pl.BlockSpec((SUBLANES, d), lambda l, j: (0, 0)),
                  pl.BlockSpec((None, d, r), lambda l, j: (l, 0, 0)),
                  pl.BlockSpec((None, r, tn), lambda l, j: (l, 0, j)),
                  pl.BlockSpec((None, 1, tn), lambda l, j: (l, 0, j))],
        out_specs=pl.BlockSpec((None, SUBLANES, tn), lambda l, j: (l, 0, j)),
        out_shape=jax.ShapeDtypeStruct((depth, SUBLANES, n), F32),
        scratch_shapes=[pltpu.VMEM((SUBLANES, r), F32)],
        compiler_params=_params("parallel", "arbitrary"),
        name="ada_modulation",
    )(cv, down, up, b.reshape(depth, 1, n))


def _normmod_kernel(t_ref, g_ref, sh_ref, sc_ref, o_ref):
    x = t_ref[...]
    ms = jnp.mean(x * x, axis=-1, keepdims=True)
    y = x * lax.rsqrt(ms + EPS) * g_ref[...]
    o_ref[...] = (y * (1.0 + sc_ref[...]) + sh_ref[...]).astype(o_ref.dtype)


def _normmod(t, gains, g_row, mods, mod_row, k_shift, k_scale, tr):
    m, d = t.shape
    vec = lambda idx: pl.BlockSpec((None, 1, d), idx)
    return pl.pallas_call(
        _normmod_kernel,
        grid=(m // tr,),
        in_specs=[pl.BlockSpec((tr, d), lambda i: (i, 0)),
                  vec(lambda i: (g_row, 0, 0)),
                  vec(lambda i: (mod_row(i, tr) + k_shift, 0, 0)),
                  vec(lambda i: (mod_row(i, tr) + k_scale, 0, 0))],
        out_specs=pl.BlockSpec((tr, d), lambda i: (i, 0)),
        out_shape=jax.ShapeDtypeStruct((m, d), BF16),
        compiler_params=_params("parallel"),
        name="norm_modulate",
    )(t, gains, mods, mods)


def _mm_kernel(x_ref, w_ref, o_ref):
    o_ref[...] = _dot(x_ref[...], w_ref[...]).astype(o_ref.dtype)


def _w_spec(widx, k, tn, col_block):
    lead = (None,) * len(widx)
    return pl.BlockSpec(lead + (k, tn), lambda j, i: widx + (0, col_block + j))


def _mm(x, w, widx, tn, out_dtype, tm):
    m, k = x.shape
    ncols = w.shape[-1]
    return pl.pallas_call(
        _mm_kernel,
        grid=(ncols // tn, m // tm),
        in_specs=[pl.BlockSpec((tm, k), lambda j, i: (i, 0)),
                  _w_spec(widx, k, tn, 0)],
        out_specs=pl.BlockSpec((tm, tn), lambda j, i: (i, j)),
        out_shape=jax.ShapeDtypeStruct((m, ncols), out_dtype),
        compiler_params=_params("parallel", "parallel"),
        name="matmul",
    )(x, w)


def _mm_swiglu_kernel(x_ref, wa_ref, wu_ref, o_ref):
    x = x_ref[...]
    a = _dot(x, wa_ref[...])
    u = _dot(x, wu_ref[...])
    o_ref[...] = (a * jax.nn.sigmoid(a) * u).astype(o_ref.dtype)


def _mm_swiglu(x, w13, widx, tn, tm):
    m, k = x.shape
    f = w13.shape[-1] // 2
    return pl.pallas_call(
        _mm_swiglu_kernel,
        grid=(f // tn, m // tm),
        in_specs=[pl.BlockSpec((tm, k), lambda j, i: (i, 0)),
                  _w_spec(widx, k, tn, 0),
                  _w_spec(widx, k, tn, f // tn)],
        out_specs=pl.BlockSpec((tm, tn), lambda j, i: (i, j)),
        out_shape=jax.ShapeDtypeStruct((m, f), BF16),
        compiler_params=_params("parallel", "parallel"),
        name="matmul_swiglu",
    )(x, w13, w13)


def _mm_resid_kernel(x_ref, w_ref, t_ref, g_ref, o_ref, *, coef):
    y = _dot(x_ref[...], w_ref[...])
    o_ref[...] = t_ref[...] + (coef * g_ref[...]) * y


def _mm_resid(x, w, widx, t, mods, mod_row, k_gate, coef, tn, tm, m_rows):
    k = x.shape[1]
    d = t.shape[1]
    return pl.pallas_call(
        functools.partial(_mm_resid_kernel, coef=coef),
        grid=(d // tn, m_rows // tm),
        in_specs=[pl.BlockSpec((tm, k), lambda j, i: (i, 0)),
                  _w_spec(widx, k, tn, 0),
                  pl.BlockSpec((tm, tn), lambda j, i: (i, j)),
                  pl.BlockSpec((None, 1, tn), lambda j, i: (mod_row(i, tm) + k_gate, 0, j))],
        out_specs=pl.BlockSpec((tm, tn), lambda j, i: (i, j)),
        out_shape=jax.ShapeDtypeStruct((m_rows, d), F32),
        compiler_params=_params("parallel", "parallel"),
        name="matmul_gated_residual",
    )(x, w, t, mods)


def _merge_kernel(o0_ref, o1_ref, o2_ref, o3_ref, z_ref, wb_ref, g0_ref, g1_ref, g2_ref, g3_ref, out_ref):
    z = z_ref[...]
    acc = None
    for i, (o_ref, g_ref) in enumerate(((o0_ref, g0_ref), (o1_ref, g1_ref), (o2_ref, g2_ref), (o3_ref, g3_ref))):
        term = jax.nn.sigmoid(_dot(z, g_ref[...])) * _dot(o_ref[...], wb_ref[i])
        acc = term if acc is None else acc + term
    out_ref[...] = acc.astype(out_ref.dtype)


def _merge(branches, z, w_branch, gate_up2, layer, tn, tm, m_rows):
    bw = branches[0].shape[1]
    r = z.shape[1]
    nbr, d = w_branch.shape[1], w_branch.shape[3]
    row = lambda width: pl.BlockSpec((tm, width), lambda j, i: (i, 0))
    gate = lambda b: pl.BlockSpec((None, r, tn), lambda j, i: (layer, 0, b * (d // tn) + j))
    return pl.pallas_call(
        _merge_kernel,
        grid=(d // tn, m_rows // tm),
        in_specs=[row(bw)] * nbr + [row(r),
                  pl.BlockSpec((None, nbr, bw, tn), lambda j, i: (layer, 0, 0, j))]
                 + [gate(b) for b in range(nbr)],
        out_specs=pl.BlockSpec((tm, tn), lambda j, i: (i, j)),
        out_shape=jax.ShapeDtypeStruct((m_rows, d), BF16),
        compiler_params=_params("parallel", "parallel"),
        name="gated_merge",
    )(*branches, z, w_branch, *([gate_up2] * nbr))


def _prep_kernel(x_ref, g_ref, cos_ref, sin_ref, o_ref, *, seg, half, n_rope, n_scaled, qscale):
    cos = cos_ref[...]
    sin = sin_ref[...]
    ri = lax.broadcasted_iota(jnp.int32, (2 * LANES, LANES), 0) & (LANES - 1)
    ci = lax.broadcasted_iota(jnp.int32, (2 * LANES, LANES), 1)
    seg_ind = jnp.where((ri & -seg) == (ci & -seg), 1.0, 0.0).astype(BF16)
    for c in range(o_ref.shape[1] // LANES):
        sl = slice(c * LANES, (c + 1) * LANES)
        x = x_ref[:, sl]
        if c >= n_rope:
            o_ref[:, sl] = x.astype(o_ref.dtype)
            continue
        lane = lax.broadcasted_iota(jnp.int32, x.shape, 1)
        ss = x * x
        ss_hi = ss.astype(BF16)
        ss_lo = (ss - ss_hi.astype(F32)).astype(BF16)
        ms = _dot(jnp.concatenate([ss_hi, ss_lo], axis=1), seg_ind) * (1.0 / seg)
        xn = x * lax.rsqrt(ms + EPS) * g_ref[:, sl]
        lower = pltpu.roll(xn, half, 1)
        upper = pltpu.roll(xn, LANES - half, 1)
        partner = jnp.where((lane & half) == 0, upper, lower)
        y = xn * cos + partner * sin
        if c < n_scaled:
            y = y * qscale
        o_ref[:, sl] = y.astype(o_ref.dtype)


def _prep(p, n, gains, cos_t, sin_t, seg, half, n_rope, n_scaled, qscale, rope_row, tr):
    m = p.shape[0]
    return pl.pallas_call(
        functools.partial(_prep_kernel, seg=seg, half=half, n_rope=n_rope, n_scaled=n_scaled, qscale=qscale),
        grid=(m // tr,),
        in_specs=[pl.BlockSpec((tr, n), lambda i: (i, 0)),
                  pl.BlockSpec((1, n), lambda i: (0, 0)),
                  pl.BlockSpec((tr, LANES), lambda i: (rope_row(i, tr), 0)),
                  pl.BlockSpec((tr, LANES), lambda i: (rope_row(i, tr), 0))],
        out_specs=pl.BlockSpec((tr, n), lambda i: (i, 0)),
        out_shape=jax.ShapeDtypeStruct((m, n), BF16),
        compiler_params=_params("parallel"),
        name="qk_norm_rope",
    )(p, gains, cos_t, sin_t)


def _rope_tables(seq, head_dim, ident_rows):
    rows = seq // GRID_W
    row = jnp.repeat(jnp.arange(rows, dtype=jnp.int32), GRID_W)
    col = jnp.tile(jnp.arange(GRID_W, dtype=jnp.int32), rows)
    axis_dim = head_dim // 2
    inv_freq = ROPE_BASE ** (-jnp.arange(0, axis_dim, 2, dtype=F32) / axis_dim)
    ar = row.astype(F32)[:, None] * inv_freq
    ac = col.astype(F32)[:, None] * inv_freq
    cr, sr, cc, sc = jnp.cos(ar), jnp.sin(ar), jnp.cos(ac), jnp.sin(ac)
    cos_t = jnp.concatenate([cr, cr, cc, cc], axis=-1)
    sin_t = jnp.concatenate([-sr, sr, -sc, sc], axis=-1)
    reps = LANES // head_dim
    cos_t = jnp.tile(cos_t, (1, reps))
    sin_t = jnp.tile(sin_t, (1, reps))
    cos_t = jnp.concatenate([cos_t, jnp.ones((ident_rows, LANES), F32)], axis=0)
    sin_t = jnp.concatenate([sin_t, jnp.zeros((ident_rows, LANES), F32)], axis=0)
    return cos_t, sin_t


def _conv_kernel(gb_ref, gc_ref, v_ref, gcp_ref, vp_ref, gcn_ref, vn_ref, w_ref, o_ref, *, seq, ctx_len, n_lat):
    tr = gb_ref.shape[0]
    i = pl.program_id(0)
    u = gc_ref[...] * v_ref[...]
    u_before = (gcp_ref[...] * vp_ref[...])[SUBLANES - 1:SUBLANES, :]
    u_after = (gcn_ref[...] * vn_ref[...])[0:1, :]
    r = lax.broadcasted_iota(jnp.int32, (tr, 1), 0)
    gr = i * tr + r
    is_lat = gr < n_lat
    pos = jnp.where(is_lat, gr & (seq - 1), (gr - n_lat) & (ctx_len - 1))
    last = jnp.where(is_lat, seq - 1, ctx_len - 1)
    u_m = jnp.where(r == 0, u_before, pltpu.roll(u, 1, 0))
    u_m = jnp.where(pos == 0, 0.0, u_m)
    u_p = jnp.where(r == tr - 1, u_after, pltpu.roll(u, tr - 1, 0))
    u_p = jnp.where(pos == last, 0.0, u_p)
    y = u_m * w_ref[0:1, :] + u * w_ref[1:2, :] + u_p * w_ref[2:3, :]
    o_ref[...] = (gb_ref[...] * y).astype(o_ref.dtype)


def _conv(pcv, conv_w, layer, seq, ctx_len, n_lat, tr):
    m = pcv.shape[0]
    bw = pcv.shape[1] // 3
    halo = tr // SUBLANES
    n_halo = m // SUBLANES
    main = lambda c: pl.BlockSpec((tr, bw), lambda i: (i, c))
    before = lambda c: pl.BlockSpec((SUBLANES, bw), lambda i: (jnp.maximum(i * halo - 1, 0), c))
    after = lambda c: pl.BlockSpec((SUBLANES, bw), lambda i: (jnp.minimum((i + 1) * halo, n_halo - 1), c))
    return pl.pallas_call(
        functools.partial(_conv_kernel, seq=seq, ctx_len=ctx_len, n_lat=n_lat),
        grid=(m // tr,),
        in_specs=[main(0), main(1), main(2), before(1), before(2), after(1), after(2),
                  pl.BlockSpec((None, 3, bw), lambda i: (layer, 0, 0))],
        out_specs=pl.BlockSpec((tr, bw), lambda i: (i, 0)),
        out_shape=jax.ShapeDtypeStruct((m, bw), BF16),
        compiler_params=_params("parallel"),
        name="short_conv",
    )(pcv, pcv, pcv, pcv, pcv, pcv, pcv, conv_w)


def _sink_column(sink_ref, h, groups, blk):
    r = lax.broadcasted_iota(jnp.int32, (groups * blk, 1), 0)
    col = jnp.zeros((groups * blk, 1), F32)
    for g in range(groups):
        col = jnp.where((r >= g * blk) & (r < (g + 1) * blk), sink_ref[h * groups + g], col)
    return col


def _stack_heads(q, groups, hd):
    return jnp.concatenate([q[:, g * hd:(g + 1) * hd] for g in range(groups)], axis=0)


def _swa_lat_kernel(sink_ref, q_ref, kp_ref, kc_ref, kn_ref, vp_ref, vc_ref, vn_ref, kx_ref, vx_ref, o_ref,
                    *, seq, groups, hd, scale, blk):
    n0 = pl.program_id(1) * (q_ref.shape[0] // blk)
    h = pl.program_id(2)
    k_all = jnp.concatenate([kp_ref[...], kc_ref[...], kn_ref[...]], axis=0)
    v_all = jnp.concatenate([vp_ref[...], vc_ref[...], vn_ref[...]], axis=0)
    kx = kx_ref[...]
    vx = vx_ref[...]
    sink = _sink_column(sink_ref, h, groups, blk)
    ri = lax.broadcasted_iota(jnp.int32, (groups * blk, 3 * blk), 0)
    ci = lax.broadcasted_iota(jnp.int32, (groups * blk, 3 * blk), 1)
    in_window = jnp.abs(ci - blk - (ri & (blk - 1))) <= SWA_WINDOW
    for qb in range(q_ref.shape[0] // blk):
        rows = slice(qb * blk, (qb + 1) * blk)
        q = _stack_heads(q_ref[rows, :], groups, hd)
        kw = k_all[qb * blk:(qb + 3) * blk]
        vw = v_all[qb * blk:(qb + 3) * blk]
        s_loc = lax.dot_general(q, kw, _NT_DIMS, preferred_element_type=F32) * scale
        s_ctx = lax.dot_general(q, kx, _NT_DIMS, preferred_element_type=F32) * scale
        kpos = (n0 + qb - 1) * blk + ci
        valid = in_window & (kpos >= 0) & (kpos < seq)
        s_loc = jnp.where(valid, s_loc, NEG_INF)
        m = jnp.maximum(jnp.maximum(jnp.max(s_ctx, axis=-1, keepdims=True),
                                    jnp.max(s_loc, axis=-1, keepdims=True)), sink)
        e_ctx = jnp.exp(s_ctx - m)
        e_loc = jnp.exp(s_loc - m)
        den = (jnp.sum(e_ctx, axis=-1, keepdims=True) + jnp.sum(e_loc, axis=-1, keepdims=True)
               + jnp.exp(sink - m))
        o = (_dot(e_ctx.astype(BF16), vx) + _dot(e_loc.astype(BF16), vw)) / den
        for g in range(groups):
            o_ref[rows, g * hd:(g + 1) * hd] = o[g * blk:(g + 1) * blk].astype(o_ref.dtype)


def _swa_ctx_kernel(sink_ref, q_ref, kx_ref, vx_ref, o_ref, *, groups, hd, scale):
    h = pl.program_id(1)
    blk = q_ref.shape[0]
    q = _stack_heads(q_ref[...], groups, hd)
    s = lax.dot_general(q, kx_ref[...], _NT_DIMS, preferred_element_type=F32) * scale
    sink = _sink_column(sink_ref, h, groups, blk)
    m = jnp.maximum(jnp.max(s, axis=-1, keepdims=True), sink)
    e = jnp.exp(s - m)
    den = jnp.sum(e, axis=-1, keepdims=True) + jnp.exp(sink - m)
    o = _dot(e.astype(BF16), vx_ref[...]) / den
    for g in range(groups):
        o_ref[:, g * hd:(g + 1) * hd] = o[g * blk:(g + 1) * blk].astype(o_ref.dtype)


def _swa(qkv, sink, hd, batch, seq, ctx_len):
    bw = SWA_HEADS * hd
    groups = SWA_HEADS // SWA_KV_HEADS
    gw = groups * hd
    scale = hd ** -0.5
    blk = SWA_BLOCK
    nb = seq // blk
    kcol = bw // hd
    vcol = kcol + SWA_KV_HEADS
    ctx_row = batch * seq // ctx_len
    smem = pl.BlockSpec(memory_space=pltpu.SMEM)

    qb = min(SWA_Q_BLOCKS, nb)
    ng = nb // qb

    def halo(col, first):
        def idx(b, n, h):
            return (b * nb + jnp.clip(n * qb + first, 0, nb - 1), col + h)
        return pl.BlockSpec((blk, hd), idx)

    own = lambda col: pl.BlockSpec((qb * blk, hd), lambda b, n, h: (b * ng + n, col + h))
    ctx_kv = lambda col: pl.BlockSpec((ctx_len, hd), lambda b, n, h: (ctx_row + b, col + h))
    lat = pl.pallas_call(
        functools.partial(_swa_lat_kernel, seq=seq, groups=groups, hd=hd, scale=scale, blk=blk),
        grid=(batch, ng, SWA_KV_HEADS),
        in_specs=[smem, pl.BlockSpec((qb * blk, gw), lambda b, n, h: (b * ng + n, h)),
                  halo(kcol, -1), own(kcol), halo(kcol, qb), halo(vcol, -1), own(vcol), halo(vcol, qb),
                  ctx_kv(kcol), ctx_kv(vcol)],
        out_specs=pl.BlockSpec((qb * blk, gw), lambda b, n, h: (b * ng + n, h)),
        out_shape=jax.ShapeDtypeStruct((batch * seq, bw), BF16),
        compiler_params=_params("parallel", "parallel", "parallel"),
        name="swa_latent",
    )(sink, qkv, qkv, qkv, qkv, qkv, qkv, qkv, qkv, qkv)
    ctx_kv2 = lambda col: pl.BlockSpec((ctx_len, hd), lambda b, h: (ctx_row + b, col + h))
    ctx = pl.pallas_call(
        functools.partial(_swa_ctx_kernel, groups=groups, hd=hd, scale=scale),
        grid=(batch, SWA_KV_HEADS),
        in_specs=[smem, pl.BlockSpec((ctx_len, gw), lambda b, h: (ctx_row + b, h)),
                  ctx_kv2(kcol), ctx_kv2(vcol)],
        out_specs=pl.BlockSpec((ctx_len, gw), lambda b, h: (b, h)),
        out_shape=jax.ShapeDtypeStruct((batch * ctx_len, bw), BF16),
        compiler_params=_params("parallel", "parallel"),
        name="swa_context",
    )(sink, qkv, qkv, qkv)
    return lat, ctx


def _vt_kernel(v_ref, o_ref):
    heads, vrows, tr = o_ref.shape
    hd2 = v_ref.shape[1] // heads
    for h in range(heads):
        o_ref[h, :hd2, :] = v_ref[:, h * hd2:(h + 1) * hd2].T.astype(o_ref.dtype)
        o_ref[h, hd2:, :] = jnp.ones((vrows - hd2, tr), o_ref.dtype)


def _vt(p, col_block, heads, tr):
    m = p.shape[0]
    hd2 = LANES
    vrows = hd2 + 2 * SUBLANES
    return pl.pallas_call(
        _vt_kernel,
        grid=(m // tr,),
        in_specs=[pl.BlockSpec((tr, heads * hd2), lambda i: (i, col_block))],
        out_specs=pl.BlockSpec((heads, vrows, tr), lambda i: (0, 0, i)),
        out_shape=jax.ShapeDtypeStruct((heads, vrows, m), BF16),
        compiler_params=_params("parallel"),
        name="v_transpose",
    )(p)


def _diff_kernel(dl_ref, g_ref, q_ref, kx_ref, vx_ref, *rest, lam_init, tk, with_latent, exp_dtype):
    if with_latent:
        k_ref, v_ref, o_ref, m_ref, acc_ref, s0_ref, s1_ref = rest
    else:
        o_ref, m_ref, acc_ref = rest
    tq, hd2 = q_ref.shape
    hd = hd2 // 2
    qf = q_ref[...].astype(F32)
    lane = lax.broadcasted_iota(jnp.int32, qf.shape, 1)
    qcat = jnp.concatenate([jnp.where(lane < hd, qf, 0.0), jnp.where(lane < hd, 0.0, qf)], axis=0).astype(BF16)

    def scores(k):
        return lax.dot_general(k, qcat, _NT_DIMS, preferred_element_type=F32)

    def absorb(s, vt):
        m_old = m_ref[...]
        m_new = jnp.maximum(m_old, jnp.max(s, axis=0, keepdims=True))
        alpha = jnp.exp2(m_old - m_new)
        p = jnp.exp2((s - m_new).astype(exp_dtype)).astype(BF16)
        acc_ref[...] = alpha * acc_ref[...] + _dot(vt, p)
        m_ref[...] = m_new

    m_ref[...] = jnp.full(m_ref.shape, NEG_INF, F32)
    acc_ref[...] = jnp.zeros(acc_ref.shape, F32)
    if with_latent:
        def chunk_rows(c):
            off = c * tk
            return pl.ds(off if isinstance(off, int) else pl.multiple_of(off, tk), tk)

        def keys(c):
            return k_ref[chunk_rows(c), :]

        def vals(c):
            return v_ref[:, chunk_rows(c)]

        n_pairs = k_ref.shape[0] // (2 * tk)
        s0_ref[...] = scores(keys(0))
        absorb(scores(kx_ref[...]), vx_ref[...])

        def body(c2, carry):
            c = 2 * c2
            s1_ref[...] = scores(keys(c + 1))
            absorb(s0_ref[...], vals(c))
            s0_ref[...] = scores(keys(c + 2))
            absorb(s1_ref[...], vals(c + 1))
            return carry
        lax.fori_loop(0, n_pairs - 1, body, 0)
        c = 2 * (n_pairs - 1)
        s1_ref[...] = scores(keys(c + 1))
        absorb(s0_ref[...], vals(c))
        absorb(s1_ref[...], vals(c + 1))
    else:
        absorb(scores(kx_ref[...]), vx_ref[...])
    acc = acc_ref[...]
    on = acc[:hd2] / acc[hd2:hd2 + 1]
    dl = dl_ref[...]
    lam = (jnp.exp(jnp.sum(dl[0:1] * dl[1:2], axis=-1, keepdims=True))
           - jnp.exp(jnp.sum(dl[2:3] * dl[3:4], axis=-1, keepdims=True)) + lam_init)
    o = (on[:, :tq] - lam * on[:, tq:]).T
    y = o * lax.rsqrt(jnp.mean(o * o, axis=-1, keepdims=True) + EPS) * g_ref[...]
    o_ref[...] = (y * (1.0 - lam_init)).astype(o_ref.dtype)


def _diff(qk, vt, diff_lambda, subln, layer, lam_init, batch, seq, ctx_len, tq, tk, exp_dtype):
    bw = qk.shape[1] // 2
    hd2 = bw // DIFF_HEADS
    vrows = vt.shape[1]
    kcol = DIFF_HEADS
    ctx_row = batch * seq // ctx_len
    nq = seq // tq
    dl_spec3 = pl.BlockSpec((None,) + diff_lambda.shape[1:], lambda b, h, i: (layer, 0, 0))
    g_spec3 = pl.BlockSpec((None, 1, hd2), lambda b, h, i: (layer, 0, 0))
    lat = pl.pallas_call(
        functools.partial(_diff_kernel, lam_init=lam_init, tk=tk, with_latent=True, exp_dtype=exp_dtype),
        grid=(batch, DIFF_HEADS, nq),
        in_specs=[dl_spec3, g_spec3,
                  pl.BlockSpec((tq, hd2), lambda b, h, i: (b * nq + i, h)),
                  pl.BlockSpec((ctx_len, hd2), lambda b, h, i: (ctx_row + b, kcol + h)),
                  pl.BlockSpec((None, vrows, ctx_len), lambda b, h, i: (h, 0, ctx_row + b)),
                  pl.BlockSpec((seq, hd2), lambda b, h, i: (b, kcol + h)),
                  pl.BlockSpec((None, vrows, seq), lambda b, h, i: (h, 0, b))],
        out_specs=pl.BlockSpec((tq, hd2), lambda b, h, i: (b * nq + i, h)),
        out_shape=jax.ShapeDtypeStruct((batch * seq, bw), BF16),
        scratch_shapes=[pltpu.VMEM((1, 2 * tq), F32), pltpu.VMEM((vrows, 2 * tq), F32),
                        pltpu.VMEM((tk, 2 * tq), F32), pltpu.VMEM((tk, 2 * tq), F32)],
        compiler_params=_params("parallel", "parallel", "arbitrary"),
        name="diff_latent",
    )(diff_lambda, subln, qk, qk, vt, qk, vt)
    dl_spec2 = pl.BlockSpec((None,) + diff_lambda.shape[1:], lambda b, h: (layer, 0, 0))
    g_spec2 = pl.BlockSpec((None, 1, hd2), lambda b, h: (layer, 0, 0))
    ctx = pl.pallas_call(
        functools.partial(_diff_kernel, lam_init=lam_init, tk=tk, with_latent=False, exp_dtype=F32),
        grid=(batch, DIFF_HEADS),
        in_specs=[dl_spec2, g_spec2,
                  pl.BlockSpec((ctx_len, hd2), lambda b, h: (ctx_row + b, h)),
                  pl.BlockSpec((ctx_len, hd2), lambda b, h: (ctx_row + b, kcol + h)),
                  pl.BlockSpec((None, vrows, ctx_len), lambda b, h: (h, 0, ctx_row + b))],
        out_specs=pl.BlockSpec((ctx_len, hd2), lambda b, h: (b, h)),
        out_shape=jax.ShapeDtypeStruct((batch * ctx_len, bw), BF16),
        scratch_shapes=[pltpu.VMEM((1, 2 * ctx_len), F32), pltpu.VMEM((vrows, 2 * ctx_len), F32)],
        compiler_params=_params("parallel", "parallel"),
        name="diff_context",
    )(diff_lambda, subln, qk, qk, vt)
    return lat, ctx


def _f1_kernel(f_ref, u_ref, a_ref):
    l1 = u_ref.shape[0]
    r = _dot(f_ref[...], u_ref[...])
    a_ref[0] = r[:l1].astype(a_ref.dtype)
    a_ref[1] = r[l1:].astype(a_ref.dtype)


def _f2_kernel(g_ref, a_ref, y_ref):
    n = a_ref.shape[1]
    r = _dot(g_ref[:, :n], a_ref[0]) + _dot(g_ref[:, n:], a_ref[1])
    y_ref[0] = r[:n].astype(y_ref.dtype)
    y_ref[1] = r[n:].astype(y_ref.dtype)


def _f3_kernel(yr_ref, yi_ref, c_ref, s_ref, o_ref):
    o_ref[...] = (_dot(yr_ref[...], c_ref[...]) + _dot(yi_ref[...], s_ref[...])).astype(o_ref.dtype)


def _fctx_kernel(f_ref, u_ref, c_ref, s_ref, o_ref):
    n = u_ref.shape[0]
    w = _dot(f_ref[...], u_ref[...])
    wr = w[:n].astype(BF16)
    wi = w[n:].astype(BF16)
    o_ref[...] = (_dot(wr, c_ref[...]) + _dot(wi, s_ref[...])).astype(o_ref.dtype)


def _angle(prod, n):
    return (2.0 * math.pi / n) * (prod % n).astype(F32)


def _dft_pair(n):
    k = jnp.arange(n, dtype=jnp.int32)
    ang = _angle(k[:, None] * k[None, :], n)
    return jnp.concatenate([jnp.cos(ang), -jnp.sin(ang)], axis=0)


def _fourier_tables(seq, ctx_len, gdim):
    l1 = seq // FFT_INNER
    f1 = _dft_pair(l1).astype(BF16)
    k1 = jnp.arange(l1, dtype=jnp.int32)[:, None, None]
    k2 = jnp.arange(FFT_INNER, dtype=jnp.int32)[None, :, None]
    t2 = jnp.arange(FFT_INNER, dtype=jnp.int32)[None, None, :]
    ang = _angle(t2 * (k1 + l1 * k2), seq)
    c, s = jnp.cos(ang), jnp.sin(ang)
    g2 = jnp.concatenate([jnp.concatenate([c, s], axis=2), jnp.concatenate([-s, c], axis=2)], axis=1).astype(BF16)
    kc = jnp.arange(gdim, dtype=jnp.int32)
    angc = _angle(kc[:, None] * kc[None, :], gdim)
    cc, sc = jnp.cos(angc), jnp.sin(angc)
    nl = 1.0 / math.sqrt(seq * gdim)
    nc = 1.0 / math.sqrt(ctx_len * gdim)
    return dict(f1=f1, g2=g2, c3=(cc * nl).astype(BF16), s3=(sc * nl).astype(BF16),
                fc=_dft_pair(ctx_len).astype(BF16), c3c=(cc * nc).astype(BF16), s3c=(sc * nc).astype(BF16))


def _fourier(pf, tabs, batch, seq, ctx_len):
    m, bw = pf.shape
    gdim = bw // FOURIER_GROUPS
    l1 = seq // FFT_INNER
    flat = FFT_INNER * bw
    cc1 = min(flat, 8192)
    a = pl.pallas_call(
        _f1_kernel,
        grid=(batch, flat // cc1),
        in_specs=[pl.BlockSpec((2 * l1, l1), lambda b, j: (0, 0)),
                  pl.BlockSpec((l1, cc1), lambda b, j: (b, j))],
        out_specs=pl.BlockSpec((None, 2, l1, cc1), lambda b, j: (b, 0, 0, j)),
        out_shape=jax.ShapeDtypeStruct((batch, 2, l1, flat), BF16),
        compiler_params=_params("parallel", "parallel"),
        name="fourier_stage1",
    )(tabs["f1"], pf.reshape(m // FFT_INNER, flat))
    y = pl.pallas_call(
        _f2_kernel,
        grid=(batch, l1),
        in_specs=[pl.BlockSpec((None, 2 * FFT_INNER, 2 * FFT_INNER), lambda b, k: (k, 0, 0)),
                  pl.BlockSpec((None, 2, None, FFT_INNER, bw), lambda b, k: (b, 0, k, 0, 0))],
        out_specs=pl.BlockSpec((None, 2, FFT_INNER, bw), lambda b, k: (b, 0, 0, k)),
        out_shape=jax.ShapeDtypeStruct((batch, 2, FFT_INNER, l1 * bw), BF16),
        compiler_params=_params("parallel", "parallel"),
        name="fourier_stage2",
    )(tabs["g2"], a.reshape(batch, 2, l1, FFT_INNER, bw))
    y4 = y.reshape(batch, 2, seq, bw)
    tm3 = min(seq, 1024)
    nt = seq // tm3
    tab = pl.BlockSpec((gdim, gdim), lambda b, i, g: (0, 0))
    lat = pl.pallas_call(
        _f3_kernel,
        grid=(batch, nt, FOURIER_GROUPS),
        in_specs=[pl.BlockSpec((None, None, tm3, gdim), lambda b, i, g: (b, 0, i, g)),
                  pl.BlockSpec((None, None, tm3, gdim), lambda b, i, g: (b, 1, i, g)), tab, tab],
        out_specs=pl.BlockSpec((tm3, gdim), lambda b, i, g: (b * nt + i, g)),
        out_shape=jax.ShapeDtypeStruct((batch * seq, bw), BF16),
        compiler_params=_params("parallel", "parallel", "parallel"),
        name="fourier_stage3",
    )(y4, y4, tabs["c3"], tabs["s3"])
    ctx_row = batch * seq // ctx_len
    tab2 = pl.BlockSpec((gdim, gdim), lambda b, g: (0, 0))
    ctx = pl.pallas_call(
        _fctx_kernel,
        grid=(batch, FOURIER_GROUPS),
        in_specs=[pl.BlockSpec((2 * ctx_len, ctx_len), lambda b, g: (0, 0)),
                  pl.BlockSpec((ctx_len, gdim), lambda b, g: (ctx_row + b, g)), tab2, tab2],
        out_specs=pl.BlockSpec((ctx_len, gdim), lambda b, g: (b, g)),
        out_shape=jax.ShapeDtypeStruct((batch * ctx_len, bw), BF16),
        compiler_params=_params("parallel", "parallel"),
        name="fourier_context",
    )(tabs["fc"], pf, tabs["c3c"], tabs["s3c"])
    return lat, ctx


def _pick(n, prefs):
    for t in prefs:
        if n % t == 0:
            return t
    return n


def kernel(x, c, ctx, c_ctx, norm_g, ada_down, ada_up, ada_b, ffn_w13, ffn_w2, w_in, conv_w, swa_q_norm,
           swa_k_norm, swa_sink, diff_q_norm, diff_k_norm, diff_lambda, diff_subln, w_branch, gate_up, w_out):
    batch, seq, d = x.shape
    ctx_len = ctx.shape[1]
    depth = norm_g.shape[0]
    bw = d // 4
    rank = gate_up.shape[1]
    d_ff = ffn_w2.shape[2]
    n_lat = batch * seq
    n_ctx = batch * ctx_len
    m = n_lat + n_ctx
    tm, tr = ROW_TILE, ELT_ROW_TILE
    swa_hd = bw // SWA_HEADS
    diff_hd = bw // (2 * DIFF_HEADS)
    assert swa_hd == LANES and 2 * diff_hd == LANES
    assert seq % tm == 0 and n_ctx % tm == 0 and seq % FFT_INNER == 0 and seq % GRID_W == 0
    assert seq & (seq - 1) == 0 and ctx_len & (ctx_len - 1) == 0 and n_lat % ctx_len == 0
    assert w_in.shape[2] == 8 * bw + 2 * SWA_KV_HEADS * swa_hd + rank

    def mod_row_for(layer):
        def mod_row(i, tile):
            who = jnp.minimum(i // (seq // tile), batch)
            return (layer * (batch + 1) + who) * N_MOD
        return mod_row

    def rope_row(i, tile):
        return jnp.where(i < n_lat // tile, i % (seq // tile), seq // tile)

    cv = jnp.concatenate([c, c_ctx[None, :], jnp.zeros((SUBLANES - batch - 1, d), F32)], axis=0)
    mods = _ada(cv, ada_down, ada_up, ada_b)[:, :batch + 1]
    mods = mods.reshape(depth * (batch + 1) * N_MOD, 1, d)
    gains = norm_g.reshape(depth * 3, 1, d)

    w13 = ffn_w13.astype(BF16)
    w2 = ffn_w2.astype(BF16)
    w_branch_b = w_branch.astype(BF16)
    gate_up_b = gate_up.astype(BF16).reshape(depth, rank, -1)
    w_out_b = w_out.astype(BF16)
    kv_w = SWA_KV_HEADS * swa_hd
    sw_w = bw + 2 * kv_w
    cuts = (0, bw, 4 * bw, 4 * bw + sw_w, 7 * bw + sw_w, 7 * bw + sw_w + rank)
    w_f, w_cv, w_sw, w_df, w_z = (w_in[:, :, a:b].astype(BF16) for a, b in zip(cuts[:-1], cuts[1:]))

    cos_s, sin_s = _rope_tables(seq, swa_hd, tr)
    cos_d, sin_d = _rope_tables(seq, diff_hd, tr)
    tabs = _fourier_tables(seq, ctx_len, bw // FOURIER_GROUPS)

    tn_ff = _pick(d_ff, COL_TILES)
    tn_d = _pick(d, COL_TILES)
    tn_bw = _pick(bw, COL_TILES)
    tn_sw = _pick(sw_w, (768,) + COL_TILES)
    swa_gains = jnp.concatenate([jnp.tile(swa_q_norm, (1, SWA_HEADS)), jnp.tile(swa_k_norm, (1, SWA_KV_HEADS)),
                                 jnp.ones((depth, kv_w), F32)], axis=1).reshape(depth, 1, sw_w)
    diff_gains = jnp.concatenate([jnp.tile(diff_q_norm, (1, 2 * DIFF_HEADS)),
                                  jnp.tile(diff_k_norm, (1, 2 * DIFF_HEADS))], axis=1).reshape(depth, 1, 2 * bw)
    subln = diff_subln.reshape(depth, 1, 2 * diff_hd)

    t = jnp.concatenate([x.reshape(n_lat, d), ctx.reshape(n_ctx, d)], axis=0)

    def ffn_half(t, layer, sub, k0, rows):
        mod_row = mod_row_for(layer)
        h = _normmod(t, gains, layer * 3 + 2 * sub, mods, mod_row, k0, k0 + 1, tr)
        act = _mm_swiglu(h, w13, (layer, sub), tn_ff, tm)
        return _mm_resid(act, w2, (layer, sub), t, mods, mod_row, k0 + 2, 0.5, tn_d, tm, rows)

    for layer in range(depth):
        last = layer == depth - 1
        rows = n_lat if last else m
        mod_row = mod_row_for(layer)
        lam_init = 0.8 - 0.6 * math.exp(-0.3 * layer)
        t = ffn_half(t, layer, 0, 0, m)
        h = _normmod(t, gains, layer * 3 + 1, mods, mod_row, 3, 4, tr)
        pf = _mm(h, w_f, (layer,), tn_bw, BF16, tm)
        pcv = _mm(h, w_cv, (layer,), tn_bw, F32, tm)
        psw = _mm(h, w_sw, (layer,), tn_sw, F32, tm)
        pdf = _mm(h, w_df, (layer,), tn_bw, F32, tm)
        pz = _mm(h, w_z, (layer,), rank, BF16, tm)

        o_f = jnp.concatenate(_fourier(pf, tabs, batch, seq, ctx_len), axis=0)
        o_c = _conv(pcv, conv_w, layer, seq, ctx_len, n_lat, tr)
        qkv_s = _prep(psw, sw_w, swa_gains[layer], cos_s, sin_s, swa_hd, swa_hd // 4,
                      SWA_HEADS + SWA_KV_HEADS, 0, 1.0, rope_row, tr)
        o_s = jnp.concatenate(_swa(qkv_s, swa_sink[layer], swa_hd, batch, seq, ctx_len), axis=0)
        qk_d = _prep(pdf, 2 * bw, diff_gains[layer], cos_d, sin_d, diff_hd, diff_hd // 4,
                     2 * DIFF_HEADS, DIFF_HEADS, diff_hd ** -0.5 * LOG2_E, rope_row, tr)
        vt_d = _vt(pdf, 2, DIFF_HEADS, tr)
        tq_d, tk_d, exp_dtype = DIFF_TRIALS[layer % len(DIFF_TRIALS)]
        o_d = jnp.concatenate(_diff(qk_d, vt_d, diff_lambda, subln, layer, lam_init, batch, seq, ctx_len,
                                    min(seq, tq_d), min(seq // 2, tk_d), exp_dtype), axis=0)

        merged = _merge([o_f, o_c, o_s, o_d], pz, w_branch_b, gate_up_b, layer, tn_d, tm, rows)
        t = _mm_resid(merged, w_out_b, (layer,), t, mods, mod_row, 5, 1.0, tn_d, tm, rows)
        t = ffn_half(t, layer, 1, 6, rows)
    return t.reshape(batch, seq, d)
```

```python
import functools
import math

import jax
import jax.numpy as jnp
from jax import lax
from jax.experimental import pallas as pl
from jax.experimental.pallas import tpu as pltpu

F32 = jnp.float32
BF16 = jnp.bfloat16

EPS = 1e-6
NEG_INF = -1e30
ROPE_BASE = 10000.0
GRID_W = 64
N_MOD = 9
FOURIER_GROUPS = 4
SWA_HEADS = 8
SWA_KV_HEADS = 2
SWA_BLOCK = 128
SWA_WINDOW = 128
DIFF_HEADS = 8
LANES = 128
SUBLANES = 8
FFT_INNER = 128

VMEM_LIMIT_BYTES = 56 * 1024 * 1024
ROW_TILE = 512
ELT_ROW_TILE = 256
DIFF_TRIALS = ((1024, 1024, False, True, 1), (1024, 1024, True, True, 1), (1024, 1024, True, True, 4),
               (1024, 1024, True, False, 1))
LOG2_E = 1.4426950408889634
COL_TILES = (1024, 512, 256, 128)
SWA_Q_BLOCKS = 4

_HIGHEST = lax.Precision.HIGHEST
_NT_DIMS = (((1,), (1,)), ((), ()))


def _params(*sem):
    return pltpu.CompilerParams(dimension_semantics=sem, vmem_limit_bytes=VMEM_LIMIT_BYTES)


def _dot(a, b):
    return jnp.dot(a, b, preferred_element_type=F32)


def _ada_kernel(cv_ref, down_ref, up_ref, b_ref, o_ref, hid_ref):
    @pl.when(pl.program_id(1) == 0)
    def _():
        c = cv_ref[...]
        s = c * jax.nn.sigmoid(c)
        hid_ref[...] = jnp.dot(s, down_ref[...], precision=_HIGHEST, preferred_element_type=F32)

    o_ref[...] = jnp.dot(hid_ref[...], up_ref[...], precision=_HIGHEST,
                         preferred_element_type=F32) + b_ref[...]


def _ada(cv, down, up, b):
    depth, d, r = down.shape
    n = up.shape[2]
    tn = n // N_MOD
    return pl.pallas_call(
        _ada_kernel,
        grid=(depth, n // tn),
        in_specs=[pl.BlockSpec((SUBLANES, d), lambda l, j: (0, 0)),
                  pl.BlockSpec((None, d, r), lambda l, j: (l, 0, 0)),
                  pl.BlockSpec((None, r, tn), lambda l, j: (l, 0, j)),
                  pl.BlockSpec((None, 1, tn), lambda l, j: (l, 0, j))],
        out_specs=pl.BlockSpec((None, SUBLANES, tn), lambda l, j: (l, 0, j)),
        out_shape=jax.ShapeDtypeStruct((depth, SUBLANES, n), F32),
        scratch_shapes=[pltpu.VMEM((SUBLANES, r), F32)],
        compiler_params=_params("parallel", "arbitrary"),
        name="ada_modulation",
    )(cv, down, up, b.reshape(depth, 1, n))


def _normmod_kernel(t_ref, g_ref, sh_ref, sc_ref, o_ref):
    x = t_ref[...]
    ms = jnp.mean(x * x, axis=-1, keepdims=True)
    y = x * lax.rsqrt(ms + EPS) * g_ref[...]
    o_ref[...] = (y * (1.0 + sc_ref[...]) + sh_ref[...]).astype(o_ref.dtype)


def _normmod(t, gains, g_row, mods, mod_row, k_shift, k_scale, tr):
    m, d = t.shape
    vec = lambda idx: pl.BlockSpec((None, 1, d), idx)
    return pl.pallas_call(
        _normmod_kernel,
        grid=(m // tr,),
        in_specs=[pl.BlockSpec((tr, d), lambda i: (i, 0)),
                  vec(lambda i: (g_row, 0, 0)),
                  vec(lambda i: (mod_row(i, tr) + k_shift, 0, 0)),
                  vec(lambda i: (mod_row(i, tr) + k_scale, 0, 0))],
        out_specs=pl.BlockSpec((tr, d), lambda i: (i, 0)),
        out_shape=jax.ShapeDtypeStruct((m, d), BF16),
        compiler_params=_params("parallel"),
        name="norm_modulate",
    )(t, gains, mods, mods)


def _mm_kernel(x_ref, w_ref, o_ref):
    o_ref[...] = _dot(x_ref[...], w_ref[...]).astype(o_ref.dtype)


def _w_spec(widx, k, tn, col_block):
    lead = (None,) * len(widx)
    return pl.BlockSpec(lead + (k, tn), lambda j, i: widx + (0, col_block + j))


def _mm(x, w, widx, tn, out_dtype, tm):
    m, k = x.shape
    ncols = w.shape[-1]
    return pl.pallas_call(
        _mm_kernel,
        grid=(ncols // tn, m // tm),
        in_specs=[pl.BlockSpec((tm, k), lambda j, i: (i, 0)),
                  _w_spec(widx, k, tn, 0)],
        out_specs=pl.BlockSpec((tm, tn), lambda j, i: (i, j)),
        out_shape=jax.ShapeDtypeStruct((m, ncols), out_dtype),
        compiler_params=_params("parallel", "parallel"),
        name="matmul",
    )(x, w)


def _mm_swiglu_kernel(x_ref, wa_ref, wu_ref, o_ref):
    x = x_ref[...]
    a = _dot(x, wa_ref[...])
    u = _dot(x, wu_ref[...])
    o_ref[...] = (a * jax.nn.sigmoid(a) * u).astype(o_ref.dtype)


def _mm_swiglu(x, w13, widx, tn, tm):
    m, k = x.shape
    f = w13.shape[-1] // 2
    return pl.pallas_call(
        _mm_swiglu_kernel,
        grid=(f // tn, m // tm),
        in_specs=[pl.BlockSpec((tm, k), lambda j, i: (i, 0)),
                  _w_spec(widx, k, tn, 0),
                  _w_spec(widx, k, tn, f // tn)],
        out_specs=pl.BlockSpec((tm, tn), lambda j, i: (i, j)),
        out_shape=jax.ShapeDtypeStruct((m, f), BF16),
        compiler_params=_params("parallel", "parallel"),
        name="matmul_swiglu",
    )(x, w13, w13)


def _mm_resid_kernel(x_ref, w_ref, t_ref, g_ref, o_ref, *, coef):
    y = _dot(x_ref[...], w_ref[...])
    o_ref[...] = t_ref[...] + (coef * g_ref[...]) * y


def _mm_resid(x, w, widx, t, mods, mod_row, k_gate, coef, tn, tm, m_rows):
    k = x.shape[1]
    d = t.shape[1]
    return pl.pallas_call(
        functools.partial(_mm_resid_kernel, coef=coef),
        grid=(d // tn, m_rows // tm),
        in_specs=[pl.BlockSpec((tm, k), lambda j, i: (i, 0)),
                  _w_spec(widx, k, tn, 0),
                  pl.BlockSpec((tm, tn), lambda j, i: (i, j)),
                  pl.BlockSpec((None, 1, tn), lambda j, i: (mod_row(i, tm) + k_gate, 0, j))],
        out_specs=pl.BlockSpec((tm, tn), lambda j, i: (i, j)),
        out_shape=jax.ShapeDtypeStruct((m_rows, d), F32),
        compiler_params=_params("parallel", "parallel"),
        name="matmul_gated_residual",
    )(x, w, t, mods)


def _merge_kernel(*refs, nbr, n_lat_tiles):
    lat_refs, ctx_refs = refs[:nbr], refs[nbr:2 * nbr]
    z_ref, wb_ref = refs[2 * nbr:2 * nbr + 2]
    g_refs = refs[2 * nbr + 2:3 * nbr + 2]
    out_ref = refs[-1]

    def compute(o_refs):
        z = z_ref[...]
        acc = None
        for b in range(nbr):
            term = jax.nn.sigmoid(_dot(z, g_refs[b][...])) * _dot(o_refs[b][...], wb_ref[b])
            acc = term if acc is None else acc + term
        out_ref[...] = acc.astype(out_ref.dtype)

    is_lat = pl.program_id(1) < n_lat_tiles
    pl.when(is_lat)(lambda: compute(lat_refs))
    pl.when(jnp.logical_not(is_lat))(lambda: compute(ctx_refs))


def _merge(branches, z, w_branch, gate_up2, layer, tn, tm, m_rows, n_lat):
    bw = branches[0][0].shape[1]
    r = z.shape[1]
    nbr, d = w_branch.shape[1], w_branch.shape[3]
    nl = n_lat // tm
    lat = pl.BlockSpec((tm, bw), lambda j, i: (jnp.minimum(i, nl - 1), 0))
    ctx = lambda first: pl.BlockSpec((tm, bw), lambda j, i: (jnp.maximum(i - nl, 0) + first, 0))
    gate = lambda b: pl.BlockSpec((None, r, tn), lambda j, i: (layer, 0, b * (d // tn) + j))
    return pl.pallas_call(
        functools.partial(_merge_kernel, nbr=nbr, n_lat_tiles=nl),
        grid=(d // tn, m_rows // tm),
        in_specs=[lat] * nbr + [ctx(first) for _, _, first in branches]
                 + [pl.BlockSpec((tm, r), lambda j, i: (i, 0)),
                    pl.BlockSpec((None, nbr, bw, tn), lambda j, i: (layer, 0, 0, j))]
                 + [gate(b) for b in range(nbr)],
        out_specs=pl.BlockSpec((tm, tn), lambda j, i: (i, j)),
        out_shape=jax.ShapeDtypeStruct((m_rows, d), BF16),
        compiler_params=_params("parallel", "parallel"),
        name="gated_merge",
    )(*[a for a, _, _ in branches], *[c for _, c, _ in branches], z, w_branch, *([gate_up2] * nbr))


def _prep_kernel(x_ref, g_ref, cos_ref, sin_ref, o_ref, *, seg, half, n_rope, n_scaled, qscale):
    cos = cos_ref[...]
    sin = sin_ref[...]
    ri = lax.broadcasted_iota(jnp.int32, (2 * LANES, LANES), 0) & (LANES - 1)
    ci = lax.broadcasted_iota(jnp.int32, (2 * LANES, LANES), 1)
    seg_ind = jnp.where((ri & -seg) == (ci & -seg), 1.0, 0.0).astype(BF16)
    for c in range(o_ref.shape[1] // LANES):
        sl = slice(c * LANES, (c + 1) * LANES)
        x = x_ref[:, sl]
        if c >= n_rope:
            o_ref[:, sl] = x.astype(o_ref.dtype)
            continue
        lane = lax.broadcasted_iota(jnp.int32, x.shape, 1)
        ss = x * x
        ss_hi = ss.astype(BF16)
        ss_lo = (ss - ss_hi.astype(F32)).astype(BF16)
        ms = _dot(jnp.concatenate([ss_hi, ss_lo], axis=1), seg_ind) * (1.0 / seg)
        xn = x * lax.rsqrt(ms + EPS) * g_ref[:, sl]
        lower = pltpu.roll(xn, half, 1)
        upper = pltpu.roll(xn, LANES - half, 1)
        partner = jnp.where((lane & half) == 0, upper, lower)
        y = xn * cos + partner * sin
        if c < n_scaled:
            y = y * qscale
        o_ref[:, sl] = y.astype(o_ref.dtype)


def _prep(p, n, gains, cos_t, sin_t, seg, half, n_rope, n_scaled, qscale, rope_row, tr):
    m = p.shape[0]
    return pl.pallas_call(
        functools.partial(_prep_kernel, seg=seg, half=half, n_rope=n_rope, n_scaled=n_scaled, qscale=qscale),
        grid=(m // tr,),
        in_specs=[pl.BlockSpec((tr, n), lambda i: (i, 0)),
                  pl.BlockSpec((1, n), lambda i: (0, 0)),
                  pl.BlockSpec((tr, LANES), lambda i: (rope_row(i, tr), 0)),
                  pl.BlockSpec((tr, LANES), lambda i: (rope_row(i, tr), 0))],
        out_specs=pl.BlockSpec((tr, n), lambda i: (i, 0)),
        out_shape=jax.ShapeDtypeStruct((m, n), BF16),
        compiler_params=_params("parallel"),
        name="qk_norm_rope",
    )(p, gains, cos_t, sin_t)


def _rope_tables(seq, head_dim, ident_rows):
    rows = seq // GRID_W
    row = jnp.repeat(jnp.arange(rows, dtype=jnp.int32), GRID_W)
    col = jnp.tile(jnp.arange(GRID_W, dtype=jnp.int32), rows)
    axis_dim = head_dim // 2
    inv_freq = ROPE_BASE ** (-jnp.arange(0, axis_dim, 2, dtype=F32) / axis_dim)
    ar = row.astype(F32)[:, None] * inv_freq
    ac = col.astype(F32)[:, None] * inv_freq
    cr, sr, cc, sc = jnp.cos(ar), jnp.sin(ar), jnp.cos(ac), jnp.sin(ac)
    cos_t = jnp.concatenate([cr, cr, cc, cc], axis=-1)
    sin_t = jnp.concatenate([-sr, sr, -sc, sc], axis=-1)
    reps = LANES // head_dim
    cos_t = jnp.tile(cos_t, (1, reps))
    sin_t = jnp.tile(sin_t, (1, reps))
    cos_t = jnp.concatenate([cos_t, jnp.ones((ident_rows, LANES), F32)], axis=0)
    sin_t = jnp.concatenate([sin_t, jnp.zeros((ident_rows, LANES), F32)], axis=0)
    return cos_t, sin_t


def _conv_kernel(gb_ref, gc_ref, v_ref, gcp_ref, vp_ref, gcn_ref, vn_ref, w_ref, o_ref, *, seq, ctx_len, n_lat):
    tr = gb_ref.shape[0]
    i = pl.program_id(0)
    u = gc_ref[...] * v_ref[...]
    u_before = (gcp_ref[...] * vp_ref[...])[SUBLANES - 1:SUBLANES, :]
    u_after = (gcn_ref[...] * vn_ref[...])[0:1, :]
    r = lax.broadcasted_iota(jnp.int32, (tr, 1), 0)
    gr = i * tr + r
    is_lat = gr < n_lat
    pos = jnp.where(is_lat, gr & (seq - 1), (gr - n_lat) & (ctx_len - 1))
    last = jnp.where(is_lat, seq - 1, ctx_len - 1)
    u_m = jnp.where(r == 0, u_before, pltpu.roll(u, 1, 0))
    u_m = jnp.where(pos == 0, 0.0, u_m)
    u_p = jnp.where(r == tr - 1, u_after, pltpu.roll(u, tr - 1, 0))
    u_p = jnp.where(pos == last, 0.0, u_p)
    y = u_m * w_ref[0:1, :] + u * w_ref[1:2, :] + u_p * w_ref[2:3, :]
    o_ref[...] = (gb_ref[...] * y).astype(o_ref.dtype)


def _conv(pcv, conv_w, layer, seq, ctx_len, n_lat, tr):
    m = pcv.shape[0]
    bw = pcv.shape[1] // 3
    halo = tr // SUBLANES
    n_halo = m // SUBLANES
    main = lambda c: pl.BlockSpec((tr, bw), lambda i: (i, c))
    before = lambda c: pl.BlockSpec((SUBLANES, bw), lambda i: (jnp.maximum(i * halo - 1, 0), c))
    after = lambda c: pl.BlockSpec((SUBLANES, bw), lambda i: (jnp.minimum((i + 1) * halo, n_halo - 1), c))
    return pl.pallas_call(
        functools.partial(_conv_kernel, seq=seq, ctx_len=ctx_len, n_lat=n_lat),
        grid=(m // tr,),
        in_specs=[main(0), main(1), main(2), before(1), before(2), after(1), after(2),
                  pl.BlockSpec((None, 3, bw), lambda i: (layer, 0, 0))],
        out_specs=pl.BlockSpec((tr, bw), lambda i: (i, 0)),
        out_shape=jax.ShapeDtypeStruct((m, bw), BF16),
        compiler_params=_params("parallel"),
        name="short_conv",
    )(pcv, pcv, pcv, pcv, pcv, pcv, pcv, conv_w)


def _sink_column(sink_ref, h, groups, blk):
    r = lax.broadcasted_iota(jnp.int32, (groups * blk, 1), 0)
    col = jnp.zeros((groups * blk, 1), F32)
    for g in range(groups):
        col = jnp.where((r >= g * blk) & (r < (g + 1) * blk), sink_ref[h * groups + g], col)
    return col


def _stack_heads(q, groups, hd):
    return jnp.concatenate([q[:, g * hd:(g + 1) * hd] for g in range(groups)], axis=0)


def _swa_lat_kernel(sink_ref, q_ref, kp_ref, kc_ref, kn_ref, vp_ref, vc_ref, vn_ref, kx_ref, vx_ref, o_ref,
                    *, seq, groups, hd, scale, blk):
    n0 = pl.program_id(1) * (q_ref.shape[0] // blk)
    h = pl.program_id(2)
    k_all = jnp.concatenate([kp_ref[...], kc_ref[...], kn_ref[...]], axis=0)
    v_all = jnp.concatenate([vp_ref[...], vc_ref[...], vn_ref[...]], axis=0)
    kx = kx_ref[...]
    vx = vx_ref[...]
    sink = _sink_column(sink_ref, h, groups, blk)
    ri = lax.broadcasted_iota(jnp.int32, (groups * blk, 3 * blk), 0)
    ci = lax.broadcasted_iota(jnp.int32, (groups * blk, 3 * blk), 1)
    in_window = jnp.abs(ci - blk - (ri & (blk - 1))) <= SWA_WINDOW
    for qb in range(q_ref.shape[0] // blk):
        rows = slice(qb * blk, (qb + 1) * blk)
        q = _stack_heads(q_ref[rows, :], groups, hd)
        kw = k_all[qb * blk:(qb + 3) * blk]
        vw = v_all[qb * blk:(qb + 3) * blk]
        s_loc = lax.dot_general(q, kw, _NT_DIMS, preferred_element_type=F32) * scale
        s_ctx = lax.dot_general(q, kx, _NT_DIMS, preferred_element_type=F32) * scale
        kpos = (n0 + qb - 1) * blk + ci
        valid = in_window & (kpos >= 0) & (kpos < seq)
        s_loc = jnp.where(valid, s_loc, NEG_INF)
        m = jnp.maximum(jnp.maximum(jnp.max(s_ctx, axis=-1, keepdims=True),
                                    jnp.max(s_loc, axis=-1, keepdims=True)), sink)
        e_ctx = jnp.exp(s_ctx - m)
        e_loc = jnp.exp(s_loc - m)
        den = (jnp.sum(e_ctx, axis=-1, keepdims=True) + jnp.sum(e_loc, axis=-1, keepdims=True)
               + jnp.exp(sink - m))
        o = (_dot(e_ctx.astype(BF16), vx) + _dot(e_loc.astype(BF16), vw)) / den
        for g in range(groups):
            o_ref[rows, g * hd:(g + 1) * hd] = o[g * blk:(g + 1) * blk].astype(o_ref.dtype)


def _swa_ctx_kernel(sink_ref, q_ref, kx_ref, vx_ref, o_ref, *, groups, hd, scale):
    h = pl.program_id(1)
    blk = q_ref.shape[0]
    q = _stack_heads(q_ref[...], groups, hd)
    s = lax.dot_general(q, kx_ref[...], _NT_DIMS, preferred_element_type=F32) * scale
    sink = _sink_column(sink_ref, h, groups, blk)
    m = jnp.maximum(jnp.max(s, axis=-1, keepdims=True), sink)
    e = jnp.exp(s - m)
    den = jnp.sum(e, axis=-1, keepdims=True) + jnp.exp(sink - m)
    o = _dot(e.astype(BF16), vx_ref[...]) / den
    for g in range(groups):
        o_ref[:, g * hd:(g + 1) * hd] = o[g * blk:(g + 1) * blk].astype(o_ref.dtype)


def _swa(qkv, sink, hd, batch, seq, ctx_len):
    bw = SWA_HEADS * hd
    groups = SWA_HEADS // SWA_KV_HEADS
    gw = groups * hd
    scale = hd ** -0.5
    blk = SWA_BLOCK
    nb = seq // blk
    kcol = bw // hd
    vcol = kcol + SWA_KV_HEADS
    ctx_row = batch * seq // ctx_len
    smem = pl.BlockSpec(memory_space=pltpu.SMEM)

    qb = min(SWA_Q_BLOCKS, nb)
    ng = nb // qb

    def halo(col, first):
        def idx(b, n, h):
            return (b * nb + jnp.clip(n * qb + first, 0, nb - 1), col + h)
        return pl.BlockSpec((blk, hd), idx)

    own = lambda col: pl.BlockSpec((qb * blk, hd), lambda b, n, h: (b * ng + n, col + h))
    ctx_kv = lambda col: pl.BlockSpec((ctx_len, hd), lambda b, n, h: (ctx_row + b, col + h))
    lat = pl.pallas_call(
        functools.partial(_swa_lat_kernel, seq=seq, groups=groups, hd=hd, scale=scale, blk=blk),
        grid=(batch, ng, SWA_KV_HEADS),
        in_specs=[smem, pl.BlockSpec((qb * blk, gw), lambda b, n, h: (b * ng + n, h)),
                  halo(kcol, -1), own(kcol), halo(kcol, qb), halo(vcol, -1), own(vcol), halo(vcol, qb),
                  ctx_kv(kcol), ctx_kv(vcol)],
        out_specs=pl.BlockSpec((qb * blk, gw), lambda b, n, h: (b * ng + n, h)),
        out_shape=jax.ShapeDtypeStruct((batch * seq, bw), BF16),
        compiler_params=_params("parallel", "parallel", "parallel"),
        name="swa_latent",
    )(sink, qkv, qkv, qkv, qkv, qkv, qkv, qkv, qkv, qkv)
    ctx_kv2 = lambda col: pl.BlockSpec((ctx_len, hd), lambda b, h: (ctx_row + b, col + h))
    ctx = pl.pallas_call(
        functools.partial(_swa_ctx_kernel, groups=groups, hd=hd, scale=scale),
        grid=(batch, SWA_KV_HEADS),
        in_specs=[smem, pl.BlockSpec((ctx_len, gw), lambda b, h: (ctx_row + b, h)),
                  ctx_kv2(kcol), ctx_kv2(vcol)],
        out_specs=pl.BlockSpec((ctx_len, gw), lambda b, h: (b, h)),
        out_shape=jax.ShapeDtypeStruct((batch * ctx_len, bw), BF16),
        compiler_params=_params("parallel", "parallel"),
        name="swa_context",
    )(sink, qkv, qkv, qkv)
    return lat, ctx


def _vt_kernel(v_ref, o_ref):
    heads, vrows, tr = o_ref.shape
    hd2 = v_ref.shape[1] // heads
    for h in range(heads):
        o_ref[h, :hd2, :] = v_ref[:, h * hd2:(h + 1) * hd2].T.astype(o_ref.dtype)
        o_ref[h, hd2:, :] = jnp.ones((vrows - hd2, tr), o_ref.dtype)


def _vt(p, col_block, heads, tr):
    m = p.shape[0]
    hd2 = LANES
    vrows = hd2 + 2 * SUBLANES
    return pl.pallas_call(
        _vt_kernel,
        grid=(m // tr,),
        in_specs=[pl.BlockSpec((tr, heads * hd2), lambda i: (i, col_block))],
        out_specs=pl.BlockSpec((heads, vrows, tr), lambda i: (0, 0, i)),
        out_shape=jax.ShapeDtypeStruct((heads, vrows, m), BF16),
        compiler_params=_params("parallel"),
        name="v_transpose",
    )(p)


def _diff_kernel(dl_ref, g_ref, q_ref, kx_ref, vx_ref, *rest, lam_init, tk, with_latent, q_transposed, mxu_sum,
                 pairs_per_trip):
    if with_latent:
        k_ref, v_ref, o_ref, m_ref, l_ref, acc_ref, s0_ref, s1_ref = rest
    else:
        o_ref, m_ref, l_ref, acc_ref = rest
    tq, hd2 = q_ref.shape
    hd = hd2 // 2
    vrows = acc_ref.shape[0]
    qf = q_ref[...].astype(F32)
    if q_transposed:
        qt = qf.T
        row = lax.broadcasted_iota(jnp.int32, qt.shape, 0)
        qcat = jnp.concatenate([jnp.where(row < hd, qt, 0.0), jnp.where(row < hd, 0.0, qt)], axis=1).astype(BF16)

        def scores(k):
            return _dot(k, qcat)
    else:
        lane = lax.broadcasted_iota(jnp.int32, qf.shape, 1)
        qcat = jnp.concatenate([jnp.where(lane < hd, qf, 0.0), jnp.where(lane < hd, 0.0, qf)], axis=0).astype(BF16)

        def scores(k):
            return lax.dot_general(k, qcat, _NT_DIMS, preferred_element_type=F32)

    def absorb(s, vt):
        m_old = m_ref[...]
        m_new = jnp.maximum(m_old, jnp.max(s, axis=0, keepdims=True))
        alpha = jnp.exp2(m_old - m_new)
        p = jnp.exp2(s - m_new)
        if not mxu_sum:
            l_ref[...] = alpha * l_ref[...] + jnp.sum(p, axis=0, keepdims=True)
        acc_ref[...] = alpha * acc_ref[...] + _dot(vt[:vrows], p.astype(BF16))
        m_ref[...] = m_new

    m_ref[...] = jnp.full(m_ref.shape, NEG_INF, F32)
    l_ref[...] = jnp.zeros(l_ref.shape, F32)
    acc_ref[...] = jnp.zeros(acc_ref.shape, F32)
    if with_latent:
        def chunk_rows(c):
            off = c * tk
            return pl.ds(off if isinstance(off, int) else pl.multiple_of(off, tk), tk)

        def keys(c):
            return k_ref[chunk_rows(c), :]

        def vals(c):
            return v_ref[:, chunk_rows(c)]

        n_pairs = k_ref.shape[0] // (2 * tk)
        s0_ref[...] = scores(keys(0))
        absorb(scores(kx_ref[...]), vx_ref[...])

        def pair(p, last):
            c = 2 * p
            s1_ref[...] = scores(keys(c + 1))
            absorb(s0_ref[...], vals(c))
            if not last:
                s0_ref[...] = scores(keys(c + 2))
            absorb(s1_ref[...], vals(c + 1))

        trips = (n_pairs - 1) // pairs_per_trip

        def body(t, carry):
            for j in range(pairs_per_trip):
                pair(t * pairs_per_trip + j, False)
            return carry
        lax.fori_loop(0, trips, body, 0)
        for p in range(trips * pairs_per_trip, n_pairs):
            pair(p, p == n_pairs - 1)
    else:
        absorb(scores(kx_ref[...]), vx_ref[...])
    acc = acc_ref[...]
    on = acc[:hd2] / (acc[hd2:hd2 + 1] if mxu_sum else l_ref[...])
    dl = dl_ref[...]
    lam = (jnp.exp(jnp.sum(dl[0:1] * dl[1:2], axis=-1, keepdims=True))
           - jnp.exp(jnp.sum(dl[2:3] * dl[3:4], axis=-1, keepdims=True)) + lam_init)
    o = (on[:, :tq] - lam * on[:, tq:]).T
    y = o * lax.rsqrt(jnp.mean(o * o, axis=-1, keepdims=True) + EPS) * g_ref[...]
    o_ref[...] = (y * (1.0 - lam_init)).astype(o_ref.dtype)


def _diff(qk, vt, diff_lambda, subln, layer, lam_init, batch, seq, ctx_len, tq, tk, q_transposed, mxu_sum,
          pairs_per_trip):
    bw = qk.shape[1] // 2
    hd2 = bw // DIFF_HEADS
    vrows = vt.shape[1]
    arows = vrows if mxu_sum else hd2
    flags = dict(lam_init=lam_init, tk=tk, q_transposed=q_transposed, mxu_sum=mxu_sum,
                 pairs_per_trip=pairs_per_trip)
    kcol = DIFF_HEADS
    ctx_row = batch * seq // ctx_len
    nq = seq // tq
    dl_spec3 = pl.BlockSpec((None,) + diff_lambda.shape[1:], lambda b, h, i: (layer, 0, 0))
    g_spec3 = pl.BlockSpec((None, 1, hd2), lambda b, h, i: (layer, 0, 0))
    lat = pl.pallas_call(
        functools.partial(_diff_kernel, with_latent=True, **flags),
        grid=(batch, DIFF_HEADS, nq),
        in_specs=[dl_spec3, g_spec3,
                  pl.BlockSpec((tq, hd2), lambda b, h, i: (b * nq + i, h)),
                  pl.BlockSpec((ctx_len, hd2), lambda b, h, i: (ctx_row + b, kcol + h)),
                  pl.BlockSpec((None, vrows, ctx_len), lambda b, h, i: (h, 0, ctx_row + b)),
                  pl.BlockSpec((seq, hd2), lambda b, h, i: (b, kcol + h)),
                  pl.BlockSpec((None, vrows, seq), lambda b, h, i: (h, 0, b))],
        out_specs=pl.BlockSpec((tq, hd2), lambda b, h, i: (b * nq + i, h)),
        out_shape=jax.ShapeDtypeStruct((batch * seq, bw), BF16),
        scratch_shapes=[pltpu.VMEM((1, 2 * tq), F32), pltpu.VMEM((1, 2 * tq), F32),
                        pltpu.VMEM((arows, 2 * tq), F32),
                        pltpu.VMEM((tk, 2 * tq), F32), pltpu.VMEM((tk, 2 * tq), F32)],
        compiler_params=_params("parallel", "parallel", "arbitrary"),
        name="diff_latent",
    )(diff_lambda, subln, qk, qk, vt, qk, vt)
    dl_spec2 = pl.BlockSpec((None,) + diff_lambda.shape[1:], lambda b, h: (layer, 0, 0))
    g_spec2 = pl.BlockSpec((None, 1, hd2), lambda b, h: (layer, 0, 0))
    ctx = pl.pallas_call(
        functools.partial(_diff_kernel, with_latent=False, **flags),
        grid=(batch, DIFF_HEADS),
        in_specs=[dl_spec2, g_spec2,
                  pl.BlockSpec((ctx_len, hd2), lambda b, h: (ctx_row + b, h)),
                  pl.BlockSpec((ctx_len, hd2), lambda b, h: (ctx_row + b, kcol + h)),
                  pl.BlockSpec((None, vrows, ctx_len), lambda b, h: (h, 0, ctx_row + b))],
        out_specs=pl.BlockSpec((ctx_len, hd2), lambda b, h: (b, h)),
        out_shape=jax.ShapeDtypeStruct((batch * ctx_len, bw), BF16),
        scratch_shapes=[pltpu.VMEM((1, 2 * ctx_len), F32), pltpu.VMEM((1, 2 * ctx_len), F32),
                        pltpu.VMEM((arows, 2 * ctx_len), F32)],
        compiler_params=_params("parallel", "parallel"),
        name="diff_context",
    )(diff_lambda, subln, qk, qk, vt)
    return lat, ctx


def _f1_kernel(f_ref, u_ref, a_ref):
    l1 = u_ref.shape[0]
    r = _dot(f_ref[...], u_ref[...])
    a_ref[0] = r[:l1].astype(a_ref.dtype)
    a_ref[1] = r[l1:].astype(a_ref.dtype)


def _f2_kernel(g_ref, a_ref, y_ref):
    n = a_ref.shape[1]
    r = _dot(g_ref[:, :n], a_ref[0]) + _dot(g_ref[:, n:], a_ref[1])
    y_ref[0] = r[:n].astype(y_ref.dtype)
    y_ref[1] = r[n:].astype(y_ref.dtype)


def _f3_kernel(yr_ref, yi_ref, c_ref, s_ref, o_ref):
    o_ref[...] = (_dot(yr_ref[...], c_ref[...]) + _dot(yi_ref[...], s_ref[...])).astype(o_ref.dtype)


def _fctx_kernel(f_ref, u_ref, c_ref, s_ref, o_ref):
    n = u_ref.shape[0]
    w = _dot(f_ref[...], u_ref[...])
    wr = w[:n].astype(BF16)
    wi = w[n:].astype(BF16)
    o_ref[...] = (_dot(wr, c_ref[...]) + _dot(wi, s_ref[...])).astype(o_ref.dtype)


def _angle(prod, n):
    return (2.0 * math.pi / n) * (prod % n).astype(F32)


def _dft_pair(n):
    k = jnp.arange(n, dtype=jnp.int32)
    ang = _angle(k[:, None] * k[None, :], n)
    return jnp.concatenate([jnp.cos(ang), -jnp.sin(ang)], axis=0)


def _fourier_tables(seq, ctx_len, gdim):
    l1 = seq // FFT_INNER
    f1 = _dft_pair(l1).astype(BF16)
    k1 = jnp.arange(l1, dtype=jnp.int32)[:, None, None]
    k2 = jnp.arange(FFT_INNER, dtype=jnp.int32)[None, :, None]
    t2 = jnp.arange(FFT_INNER, dtype=jnp.int32)[None, None, :]
    ang = _angle(t2 * (k1 + l1 * k2), seq)
    c, s = jnp.cos(ang), jnp.sin(ang)
    g2 = jnp.concatenate([jnp.concatenate([c, s], axis=2), jnp.concatenate([-s, c], axis=2)], axis=1).astype(BF16)
    kc = jnp.arange(gdim, dtype=jnp.int32)
    angc = _angle(kc[:, None] * kc[None, :], gdim)
    cc, sc = jnp.cos(angc), jnp.sin(angc)
    nl = 1.0 / math.sqrt(seq * gdim)
    nc = 1.0 / math.sqrt(ctx_len * gdim)
    return dict(f1=f1, g2=g2, c3=(cc * nl).astype(BF16), s3=(sc * nl).astype(BF16),
                fc=_dft_pair(ctx_len).astype(BF16), c3c=(cc * nc).astype(BF16), s3c=(sc * nc).astype(BF16))


def _fourier(pf, tabs, batch, seq, ctx_len):
    m, bw = pf.shape
    gdim = bw // FOURIER_GROUPS
    l1 = seq // FFT_INNER
    flat = FFT_INNER * bw
    cc1 = min(flat, 8192)
    a = pl.pallas_call(
        _f1_kernel,
        grid=(batch, flat // cc1),
        in_specs=[pl.BlockSpec((2 * l1, l1), lambda b, j: (0, 0)),
                  pl.BlockSpec((l1, cc1), lambda b, j: (b, j))],
        out_specs=pl.BlockSpec((None, 2, l1, cc1), lambda b, j: (b, 0, 0, j)),
        out_shape=jax.ShapeDtypeStruct((batch, 2, l1, flat), BF16),
        compiler_params=_params("parallel", "parallel"),
        name="fourier_stage1",
    )(tabs["f1"], pf.reshape(m // FFT_INNER, flat))
    y = pl.pallas_call(
        _f2_kernel,
        grid=(batch, l1),
        in_specs=[pl.BlockSpec((None, 2 * FFT_INNER, 2 * FFT_INNER), lambda b, k: (k, 0, 0)),
                  pl.BlockSpec((None, 2, None, FFT_INNER, bw), lambda b, k: (b, 0, k, 0, 0))],
        out_specs=pl.BlockSpec((None, 2, FFT_INNER, bw), lambda b, k: (b, 0, 0, k)),
        out_shape=jax.ShapeDtypeStruct((batch, 2, FFT_INNER, l1 * bw), BF16),
        compiler_params=_params("parallel", "parallel"),
        name="fourier_stage2",
    )(tabs["g2"], a.reshape(batch, 2, l1, FFT_INNER, bw))
    y4 = y.reshape(batch, 2, seq, bw)
    tm3 = min(seq, 1024)
    nt = seq // tm3
    tab = pl.BlockSpec((gdim, gdim), lambda b, i, g: (0, 0))
    lat = pl.pallas_call(
        _f3_kernel,
        grid=(batch, nt, FOURIER_GROUPS),
        in_specs=[pl.BlockSpec((None, None, tm3, gdim), lambda b, i, g: (b, 0, i, g)),
                  pl.BlockSpec((None, None, tm3, gdim), lambda b, i, g: (b, 1, i, g)), tab, tab],
        out_specs=pl.BlockSpec((tm3, gdim), lambda b, i, g: (b * nt + i, g)),
        out_shape=jax.ShapeDtypeStruct((batch * seq, bw), BF16),
        compiler_params=_params("parallel", "parallel", "parallel"),
        name="fourier_stage3",
    )(y4, y4, tabs["c3"], tabs["s3"])
    ctx_row = batch * seq // ctx_len
    tab2 = pl.BlockSpec((gdim, gdim), lambda b, g: (0, 0))
    ctx = pl.pallas_call(
        _fctx_kernel,
        grid=(batch, FOURIER_GROUPS),
        in_specs=[pl.BlockSpec((2 * ctx_len, ctx_len), lambda b, g: (0, 0)),
                  pl.BlockSpec((ctx_len, gdim), lambda b, g: (ctx_row + b, g)), tab2, tab2],
        out_specs=pl.BlockSpec((ctx_len, gdim), lambda b, g: (b, g)),
        out_shape=jax.ShapeDtypeStruct((batch * ctx_len, bw), BF16),
        compiler_params=_params("parallel", "parallel"),
        name="fourier_context",
    )(tabs["fc"], pf, tabs["c3c"], tabs["s3c"])
    return lat, ctx


def _pick(n, prefs):
    for t in prefs:
        if n % t == 0:
            return t
    return n


def kernel(x, c, ctx, c_ctx, norm_g, ada_down, ada_up, ada_b, ffn_w13, ffn_w2, w_in, conv_w, swa_q_norm,
           swa_k_norm, swa_sink, diff_q_norm, diff_k_norm, diff_lambda, diff_subln, w_branch, gate_up, w_out):
    batch, seq, d = x.shape
    ctx_len = ctx.shape[1]
    depth = norm_g.shape[0]
    bw = d // 4
    rank = gate_up.shape[1]
    d_ff = ffn_w2.shape[2]
    n_lat = batch * seq
    n_ctx = batch * ctx_len
    m = n_lat + n_ctx
    tm, tr = ROW_TILE, ELT_ROW_TILE
    swa_hd = bw // SWA_HEADS
    diff_hd = bw // (2 * DIFF_HEADS)
    assert swa_hd == LANES and 2 * diff_hd == LANES
    assert seq % tm == 0 and n_ctx % tm == 0 and seq % FFT_INNER == 0 and seq % GRID_W == 0
    assert seq & (seq - 1) == 0 and ctx_len & (ctx_len - 1) == 0 and n_lat % ctx_len == 0
    assert w_in.shape[2] == 8 * bw + 2 * SWA_KV_HEADS * swa_hd + rank

    def mod_row_for(layer):
        def mod_row(i, tile):
            who = jnp.minimum(i // (seq // tile), batch)
            return (layer * (batch + 1) + who) * N_MOD
        return mod_row

    def rope_row(i, tile):
        return jnp.where(i < n_lat // tile, i % (seq // tile), seq // tile)

    cv = jnp.concatenate([c, c_ctx[None, :], jnp.zeros((SUBLANES - batch - 1, d), F32)], axis=0)
    mods = _ada(cv, ada_down, ada_up, ada_b)[:, :batch + 1]
    mods = mods.reshape(depth * (batch + 1) * N_MOD, 1, d)
    gains = norm_g.reshape(depth * 3, 1, d)

    w13 = ffn_w13.astype(BF16)
    w2 = ffn_w2.astype(BF16)
    w_branch_b = w_branch.astype(BF16)
    gate_up_b = gate_up.astype(BF16).reshape(depth, rank, -1)
    w_out_b = w_out.astype(BF16)
    kv_w = SWA_KV_HEADS * swa_hd
    sw_w = bw + 2 * kv_w
    cuts = (0, bw, 4 * bw, 4 * bw + sw_w, 7 * bw + sw_w, 7 * bw + sw_w + rank)
    w_f, w_cv, w_sw, w_df, w_z = (w_in[:, :, a:b].astype(BF16) for a, b in zip(cuts[:-1], cuts[1:]))

    cos_s, sin_s = _rope_tables(seq, swa_hd, tr)
    cos_d, sin_d = _rope_tables(seq, diff_hd, tr)
    tabs = _fourier_tables(seq, ctx_len, bw // FOURIER_GROUPS)

    tn_ff = _pick(d_ff, COL_TILES)
    tn_d = _pick(d, COL_TILES)
    tn_bw = _pick(bw, COL_TILES)
    tn_sw = _pick(sw_w, (768,) + COL_TILES)
    swa_gains = jnp.concatenate([jnp.tile(swa_q_norm, (1, SWA_HEADS)), jnp.tile(swa_k_norm, (1, SWA_KV_HEADS)),
                                 jnp.ones((depth, kv_w), F32)], axis=1).reshape(depth, 1, sw_w)
    diff_gains = jnp.concatenate([jnp.tile(diff_q_norm, (1, 2 * DIFF_HEADS)),
                                  jnp.tile(diff_k_norm, (1, 2 * DIFF_HEADS))], axis=1).reshape(depth, 1, 2 * bw)
    subln = diff_subln.reshape(depth, 1, 2 * diff_hd)

    t = jnp.concatenate([x.reshape(n_lat, d), ctx.reshape(n_ctx, d)], axis=0)

    def ffn_half(t, layer, sub, k0, rows):
        mod_row = mod_row_for(layer)
        h = _normmod(t, gains, layer * 3 + 2 * sub, mods, mod_row, k0, k0 + 1, tr)
        act = _mm_swiglu(h, w13, (layer, sub), tn_ff, tm)
        return _mm_resid(act, w2, (layer, sub), t, mods, mod_row, k0 + 2, 0.5, tn_d, tm, rows)

    for layer in range(depth):
        last = layer == depth - 1
        rows = n_lat if last else m
        mod_row = mod_row_for(layer)
        lam_init = 0.8 - 0.6 * math.exp(-0.3 * layer)
        t = ffn_half(t, layer, 0, 0, m)
        h = _normmod(t, gains, layer * 3 + 1, mods, mod_row, 3, 4, tr)
        pf = _mm(h, w_f, (layer,), tn_bw, BF16, tm)
        pcv = _mm(h, w_cv, (layer,), tn_bw, F32, tm)
        psw = _mm(h, w_sw, (layer,), tn_sw, F32, tm)
        pdf = _mm(h, w_df, (layer,), tn_bw, F32, tm)
        pz = _mm(h, w_z, (layer,), rank, BF16, tm)

        o_f = _fourier(pf, tabs, batch, seq, ctx_len) + (0,)
        o_c = _conv(pcv, conv_w, layer, seq, ctx_len, n_lat, tr)
        qkv_s = _prep(psw, sw_w, swa_gains[layer], cos_s, sin_s, swa_hd, swa_hd // 4,
                      SWA_HEADS + SWA_KV_HEADS, 0, 1.0, rope_row, tr)
        o_s = _swa(qkv_s, swa_sink[layer], swa_hd, batch, seq, ctx_len) + (0,)
        qk_d = _prep(pdf, 2 * bw, diff_gains[layer], cos_d, sin_d, diff_hd, diff_hd // 4,
                     2 * DIFF_HEADS, DIFF_HEADS, diff_hd ** -0.5 * LOG2_E, rope_row, tr)
        vt_d = _vt(pdf, 2, DIFF_HEADS, tr)
        tq_d, tk_d, q_transposed, mxu_sum, pairs_per_trip = DIFF_TRIALS[layer % len(DIFF_TRIALS)]
        o_d = _diff(qk_d, vt_d, diff_lambda, subln, layer, lam_init, batch, seq, ctx_len,
                    min(seq, tq_d), min(seq // 2, tk_d), q_transposed, mxu_sum, pairs_per_trip) + (0,)

        merged = _merge([o_f, (o_c, o_c, n_lat // tm), o_s, o_d], pz, w_branch_b, gate_up_b, layer,
                        tn_d, tm, rows, n_lat)
        t = _mm_resid(merged, w_out_b, (layer,), t, mods, mod_row, 5, 1.0, tn_d, tm, rows)
        t = ffn_half(t, layer, 1, 6, rows)
    return t.reshape(batch, seq, d)
```

```python
import functools
import math

import jax
import jax.numpy as jnp
from jax import lax
from jax.experimental import pallas as pl
from jax.experimental.pallas import tpu as pltpu

F32 = jnp.float32
BF16 = jnp.bfloat16

EPS = 1e-6
NEG_INF = -1e30
ROPE_BASE = 10000.0
GRID_W = 64
N_MOD = 9
FOURIER_GROUPS = 4
SWA_HEADS = 8
SWA_KV_HEADS = 2
SWA_BLOCK = 128
SWA_WINDOW = 128
DIFF_HEADS = 8
LANES = 128
SUBLANES = 8
FFT_INNER = 128

VMEM_LIMIT_BYTES = 56 * 1024 * 1024
ROW_TILE = 512
ELT_ROW_TILE = 256
DIFF_Q_TILE = 1024
DIFF_K_TILE = 1024
LOG2_E = 1.4426950408889634
COL_TILES = (1024, 512, 256, 128)
SWA_Q_BLOCKS = 4

_HIGHEST = lax.Precision.HIGHEST
_NT_DIMS = (((1,), (1,)), ((), ()))


def _params(*sem):
    return pltpu.CompilerParams(dimension_semantics=sem, vmem_limit_bytes=VMEM_LIMIT_BYTES)


def _dot(a, b):
    return jnp.dot(a, b, preferred_element_type=F32)


def _ada_kernel(cv_ref, down_ref, up_ref, b_ref, o_ref, hid_ref):
    @pl.when(pl.program_id(1) == 0)
    def _():
        c = cv_ref[...]
        s = c * jax.nn.sigmoid(c)
        hid_ref[...] = jnp.dot(s, down_ref[...], precision=_HIGHEST, preferred_element_type=F32)

    o_ref[...] = jnp.dot(hid_ref[...], up_ref[...], precision=_HIGHEST,
                         preferred_element_type=F32) + b_ref[...]


def _ada(cv, down, up, b):
    depth, d, r = down.shape
    n = up.shape[2]
    tn = n // N_MOD
    return pl.pallas_call(
        _ada_kernel,
        grid=(depth, n // tn),
        in_specs=[pl.BlockSpec((SUBLANES, d), lambda l, j: (0, 0)),
                  pl.BlockSpec((None, d, r), lambda l, j: (l, 0, 0)),
                  pl.BlockSpec((None, r, tn), lambda l, j: (l, 0, j)),
                  pl.BlockSpec((None, 1, tn), lambda l, j: (l, 0, j))],
        out_specs=pl.BlockSpec((None, SUBLANES, tn), lambda l, j: (l, 0, j)),
        out_shape=jax.ShapeDtypeStruct((depth, SUBLANES, n), F32),
        scratch_shapes=[pltpu.VMEM((SUBLANES, r), F32)],
        compiler_params=_params("parallel", "arbitrary"),
        name="ada_modulation",
    )(cv, down, up, b.reshape(depth, 1, n))


def _normmod_kernel(t_ref, g_ref, sh_ref, sc_ref, o_ref):
    x = t_ref[...]
    ms = jnp.mean(x * x, axis=-1, keepdims=True)
    y = x * lax.rsqrt(ms + EPS) * g_ref[...]
    o_ref[...] = (y * (1.0 + sc_ref[...]) + sh_ref[...]).astype(o_ref.dtype)


def _normmod(t, gains, g_row, mods, mod_row, k_shift, k_scale, tr):
    m, d = t.shape
    vec = lambda idx: pl.BlockSpec((None, 1, d), idx)
    return pl.pallas_call(
        _normmod_kernel,
        grid=(m // tr,),
        in_specs=[pl.BlockSpec((tr, d), lambda i: (i, 0)),
                  vec(lambda i: (g_row, 0, 0)),
                  vec(lambda i: (mod_row(i, tr) + k_shift, 0, 0)),
                  vec(lambda i: (mod_row(i, tr) + k_scale, 0, 0))],
        out_specs=pl.BlockSpec((tr, d), lambda i: (i, 0)),
        out_shape=jax.ShapeDtypeStruct((m, d), BF16),
        compiler_params=_params("parallel"),
        name="norm_modulate",
    )(t, gains, mods, mods)


def _mm_kernel(x_ref, w_ref, o_ref):
    o_ref[...] = _dot(x_ref[...], w_ref[...]).astype(o_ref.dtype)


def _w_spec(widx, k, tn, col_block):
    lead = (None,) * len(widx)
    return pl.BlockSpec(lead + (k, tn), lambda j, i: widx + (0, col_block + j))


def _mm(x, w, widx, tn, out_dtype, tm):
    m, k = x.shape
    ncols = w.shape[-1]
    return pl.pallas_call(
        _mm_kernel,
        grid=(ncols // tn, m // tm),
        in_specs=[pl.BlockSpec((tm, k), lambda j, i: (i, 0)),
                  _w_spec(widx, k, tn, 0)],
        out_specs=pl.BlockSpec((tm, tn), lambda j, i: (i, j)),
        out_shape=jax.ShapeDtypeStruct((m, ncols), out_dtype),
        compiler_params=_params("parallel", "parallel"),
        name="matmul",
    )(x, w)


def _mm_swiglu_kernel(x_ref, wa_ref, wu_ref, o_ref):
    x = x_ref[...]
    a = _dot(x, wa_ref[...])
    u = _dot(x, wu_ref[...])
    o_ref[...] = (a * jax.nn.sigmoid(a) * u).astype(o_ref.dtype)


def _mm_swiglu(x, w13, widx, tn, tm):
    m, k = x.shape
    f = w13.shape[-1] // 2
    return pl.pallas_call(
        _mm_swiglu_kernel,
        grid=(f // tn, m // tm),
        in_specs=[pl.BlockSpec((tm, k), lambda j, i: (i, 0)),
                  _w_spec(widx, k, tn, 0),
                  _w_spec(widx, k, tn, f // tn)],
        out_specs=pl.BlockSpec((tm, tn), lambda j, i: (i, j)),
        out_shape=jax.ShapeDtypeStruct((m, f), BF16),
        compiler_params=_params("parallel", "parallel"),
        name="matmul_swiglu",
    )(x, w13, w13)


def _mm_resid_kernel(x_ref, w_ref, t_ref, g_ref, o_ref, *, coef):
    y = _dot(x_ref[...], w_ref[...])
    o_ref[...] = t_ref[...] + (coef * g_ref[...]) * y


def _mm_resid(x, w, widx, t, mods, mod_row, k_gate, coef, tn, tm, m_rows):
    k = x.shape[1]
    d = t.shape[1]
    return pl.pallas_call(
        functools.partial(_mm_resid_kernel, coef=coef),
        grid=(d // tn, m_rows // tm),
        in_specs=[pl.BlockSpec((tm, k), lambda j, i: (i, 0)),
                  _w_spec(widx, k, tn, 0),
                  pl.BlockSpec((tm, tn), lambda j, i: (i, j)),
                  pl.BlockSpec((None, 1, tn), lambda j, i: (mod_row(i, tm) + k_gate, 0, j))],
        out_specs=pl.BlockSpec((tm, tn), lambda j, i: (i, j)),
        out_shape=jax.ShapeDtypeStruct((m_rows, d), F32),
        compiler_params=_params("parallel", "parallel"),
        name="matmul_gated_residual",
    )(x, w, t, mods)


def _merge_kernel(*refs, nbr, n_lat_tiles):
    lat_refs, ctx_refs = refs[:nbr], refs[nbr:2 * nbr]
    z_ref, wb_ref = refs[2 * nbr:2 * nbr + 2]
    g_refs = refs[2 * nbr + 2:3 * nbr + 2]
    out_ref = refs[-1]

    def compute(o_refs):
        z = z_ref[...]
        acc = None
        for b in range(nbr):
            term = jax.nn.sigmoid(_dot(z, g_refs[b][...])) * _dot(o_refs[b][...], wb_ref[b])
            acc = term if acc is None else acc + term
        out_ref[...] = acc.astype(out_ref.dtype)

    is_lat = pl.program_id(1) < n_lat_tiles
    pl.when(is_lat)(lambda: compute(lat_refs))
    pl.when(jnp.logical_not(is_lat))(lambda: compute(ctx_refs))


def _merge(branches, z, w_branch, gate_up2, layer, tn, tm, m_rows, n_lat):
    bw = branches[0][0].shape[1]
    r = z.shape[1]
    nbr, d = w_branch.shape[1], w_branch.shape[3]
    nl = n_lat // tm
    lat = pl.BlockSpec((tm, bw), lambda j, i: (jnp.minimum(i, nl - 1), 0))
    ctx = lambda first: pl.BlockSpec((tm, bw), lambda j, i: (jnp.maximum(i - nl, 0) + first, 0))
    gate = lambda b: pl.BlockSpec((None, r, tn), lambda j, i: (layer, 0, b * (d // tn) + j))
    return pl.pallas_call(
        functools.partial(_merge_kernel, nbr=nbr, n_lat_tiles=nl),
        grid=(d // tn, m_rows // tm),
        in_specs=[lat] * nbr + [ctx(first) for _, _, first in branches]
                 + [pl.BlockSpec((tm, r), lambda j, i: (i, 0)),
                    pl.BlockSpec((None, nbr, bw, tn), lambda j, i: (layer, 0, 0, j))]
                 + [gate(b) for b in range(nbr)],
        out_specs=pl.BlockSpec((tm, tn), lambda j, i: (i, j)),
        out_shape=jax.ShapeDtypeStruct((m_rows, d), BF16),
        compiler_params=_params("parallel", "parallel"),
        name="gated_merge",
    )(*[a for a, _, _ in branches], *[c for _, c, _ in branches], z, w_branch, *([gate_up2] * nbr))


def _prep_kernel(x_ref, g_ref, cos_ref, sin_ref, o_ref, *, seg, half, n_rope, n_scaled, qscale):
    cos = cos_ref[...]
    sin = sin_ref[...]
    ri = lax.broadcasted_iota(jnp.int32, (2 * LANES, LANES), 0) & (LANES - 1)
    ci = lax.broadcasted_iota(jnp.int32, (2 * LANES, LANES), 1)
    seg_ind = jnp.where((ri & -seg) == (ci & -seg), 1.0, 0.0).astype(BF16)
    for c in range(o_ref.shape[1] // LANES):
        sl = slice(c * LANES, (c + 1) * LANES)
        x = x_ref[:, sl]
        if c >= n_rope:
            o_ref[:, sl] = x.astype(o_ref.dtype)
            continue
        lane = lax.broadcasted_iota(jnp.int32, x.shape, 1)
        ss = x * x
        ss_hi = ss.astype(BF16)
        ss_lo = (ss - ss_hi.astype(F32)).astype(BF16)
        ms = _dot(jnp.concatenate([ss_hi, ss_lo], axis=1), seg_ind) * (1.0 / seg)
        xn = x * lax.rsqrt(ms + EPS) * g_ref[:, sl]
        lower = pltpu.roll(xn, half, 1)
        upper = pltpu.roll(xn, LANES - half, 1)
        partner = jnp.where((lane & half) == 0, upper, lower)
        y = xn * cos + partner * sin
        if c < n_scaled:
            y = y * qscale
        o_ref[:, sl] = y.astype(o_ref.dtype)


def _prep(p, n, gains, cos_t, sin_t, seg, half, n_rope, n_scaled, qscale, rope_row, tr):
    m = p.shape[0]
    return pl.pallas_call(
        functools.partial(_prep_kernel, seg=seg, half=half, n_rope=n_rope, n_scaled=n_scaled, qscale=qscale),
        grid=(m // tr,),
        in_specs=[pl.BlockSpec((tr, n), lambda i: (i, 0)),
                  pl.BlockSpec((1, n), lambda i: (0, 0)),
                  pl.BlockSpec((tr, LANES), lambda i: (rope_row(i, tr), 0)),
                  pl.BlockSpec((tr, LANES), lambda i: (rope_row(i, tr), 0))],
        out_specs=pl.BlockSpec((tr, n), lambda i: (i, 0)),
        out_shape=jax.ShapeDtypeStruct((m, n), BF16),
        compiler_params=_params("parallel"),
        name="qk_norm_rope",
    )(p, gains, cos_t, sin_t)


def _rope_tables(seq, head_dim, ident_rows):
    rows = seq // GRID_W
    row = jnp.repeat(jnp.arange(rows, dtype=jnp.int32), GRID_W)
    col = jnp.tile(jnp.arange(GRID_W, dtype=jnp.int32), rows)
    axis_dim = head_dim // 2
    inv_freq = ROPE_BASE ** (-jnp.arange(0, axis_dim, 2, dtype=F32) / axis_dim)
    ar = row.astype(F32)[:, None] * inv_freq
    ac = col.astype(F32)[:, None] * inv_freq
    cr, sr, cc, sc = jnp.cos(ar), jnp.sin(ar), jnp.cos(ac), jnp.sin(ac)
    cos_t = jnp.concatenate([cr, cr, cc, cc], axis=-1)
    sin_t = jnp.concatenate([-sr, sr, -sc, sc], axis=-1)
    reps = LANES // head_dim
    cos_t = jnp.tile(cos_t, (1, reps))
    sin_t = jnp.tile(sin_t, (1, reps))
    cos_t = jnp.concatenate([cos_t, jnp.ones((ident_rows, LANES), F32)], axis=0)
    sin_t = jnp.concatenate([sin_t, jnp.zeros((ident_rows, LANES), F32)], axis=0)
    return cos_t, sin_t


def _conv_kernel(gb_ref, gc_ref, v_ref, gcp_ref, vp_ref, gcn_ref, vn_ref, w_ref, o_ref, *, seq, ctx_len, n_lat):
    tr = gb_ref.shape[0]
    i = pl.program_id(0)
    u = gc_ref[...] * v_ref[...]
    u_before = (gcp_ref[...] * vp_ref[...])[SUBLANES - 1:SUBLANES, :]
    u_after = (gcn_ref[...] * vn_ref[...])[0:1, :]
    r = lax.broadcasted_iota(jnp.int32, (tr, 1), 0)
    gr = i * tr + r
    is_lat = gr < n_lat
    pos = jnp.where(is_lat, gr & (seq - 1), (gr - n_lat) & (ctx_len - 1))
    last = jnp.where(is_lat, seq - 1, ctx_len - 1)
    u_m = jnp.where(r == 0, u_before, pltpu.roll(u, 1, 0))
    u_m = jnp.where(pos == 0, 0.0, u_m)
    u_p = jnp.where(r == tr - 1, u_after, pltpu.roll(u, tr - 1, 0))
    u_p = jnp.where(pos == last, 0.0, u_p)
    y = u_m * w_ref[0:1, :] + u * w_ref[1:2, :] + u_p * w_ref[2:3, :]
    o_ref[...] = (gb_ref[...] * y).astype(o_ref.dtype)


def _conv(pcv, conv_w, layer, seq, ctx_len, n_lat, tr):
    m = pcv.shape[0]
    bw = pcv.shape[1] // 3
    halo = tr // SUBLANES
    n_halo = m // SUBLANES
    main = lambda c: pl.BlockSpec((tr, bw), lambda i: (i, c))
    before = lambda c: pl.BlockSpec((SUBLANES, bw), lambda i: (jnp.maximum(i * halo - 1, 0), c))
    after = lambda c: pl.BlockSpec((SUBLANES, bw), lambda i: (jnp.minimum((i + 1) * halo, n_halo - 1), c))
    return pl.pallas_call(
        functools.partial(_conv_kernel, seq=seq, ctx_len=ctx_len, n_lat=n_lat),
        grid=(m // tr,),
        in_specs=[main(0), main(1), main(2), before(1), before(2), after(1), after(2),
                  pl.BlockSpec((None, 3, bw), lambda i: (layer, 0, 0))],
        out_specs=pl.BlockSpec((tr, bw), lambda i: (i, 0)),
        out_shape=jax.ShapeDtypeStruct((m, bw), BF16),
        compiler_params=_params("parallel"),
        name="short_conv",
    )(pcv, pcv, pcv, pcv, pcv, pcv, pcv, conv_w)


def _sink_column(sink_ref, h, groups, blk):
    r = lax.broadcasted_iota(jnp.int32, (groups * blk, 1), 0)
    col = jnp.zeros((groups * blk, 1), F32)
    for g in range(groups):
        col = jnp.where((r >= g * blk) & (r < (g + 1) * blk), sink_ref[h * groups + g], col)
    return col


def _stack_heads(q, groups, hd):
    return jnp.concatenate([q[:, g * hd:(g + 1) * hd] for g in range(groups)], axis=0)


def _swa_lat_kernel(sink_ref, q_ref, kp_ref, kc_ref, kn_ref, vp_ref, vc_ref, vn_ref, kx_ref, vx_ref, o_ref,
                    *, seq, groups, hd, scale, blk):
    n0 = pl.program_id(1) * (q_ref.shape[0] // blk)
    h = pl.program_id(2)
    k_all = jnp.concatenate([kp_ref[...], kc_ref[...], kn_ref[...]], axis=0)
    v_all = jnp.concatenate([vp_ref[...], vc_ref[...], vn_ref[...]], axis=0)
    kx = kx_ref[...]
    vx = vx_ref[...]
    sink = _sink_column(sink_ref, h, groups, blk)
    ri = lax.broadcasted_iota(jnp.int32, (groups * blk, 3 * blk), 0)
    ci = lax.broadcasted_iota(jnp.int32, (groups * blk, 3 * blk), 1)
    in_window = jnp.abs(ci - blk - (ri & (blk - 1))) <= SWA_WINDOW
    for qb in range(q_ref.shape[0] // blk):
        rows = slice(qb * blk, (qb + 1) * blk)
        q = _stack_heads(q_ref[rows, :], groups, hd)
        kw = k_all[qb * blk:(qb + 3) * blk]
        vw = v_all[qb * blk:(qb + 3) * blk]
        s_loc = lax.dot_general(q, kw, _NT_DIMS, preferred_element_type=F32) * scale
        s_ctx = lax.dot_general(q, kx, _NT_DIMS, preferred_element_type=F32) * scale
        kpos = (n0 + qb - 1) * blk + ci
        valid = in_window & (kpos >= 0) & (kpos < seq)
        s_loc = jnp.where(valid, s_loc, NEG_INF)
        m = jnp.maximum(jnp.maximum(jnp.max(s_ctx, axis=-1, keepdims=True),
                                    jnp.max(s_loc, axis=-1, keepdims=True)), sink)
        e_ctx = jnp.exp(s_ctx - m)
        e_loc = jnp.exp(s_loc - m)
        den = (jnp.sum(e_ctx, axis=-1, keepdims=True) + jnp.sum(e_loc, axis=-1, keepdims=True)
               + jnp.exp(sink - m))
        o = (_dot(e_ctx.astype(BF16), vx) + _dot(e_loc.astype(BF16), vw)) / den
        for g in range(groups):
            o_ref[rows, g * hd:(g + 1) * hd] = o[g * blk:(g + 1) * blk].astype(o_ref.dtype)


def _swa_ctx_kernel(sink_ref, q_ref, kx_ref, vx_ref, o_ref, *, groups, hd, scale):
    h = pl.program_id(1)
    blk = q_ref.shape[0]
    q = _stack_heads(q_ref[...], groups, hd)
    s = lax.dot_general(q, kx_ref[...], _NT_DIMS, preferred_element_type=F32) * scale
    sink = _sink_column(sink_ref, h, groups, blk)
    m = jnp.maximum(jnp.max(s, axis=-1, keepdims=True), sink)
    e = jnp.exp(s - m)
    den = jnp.sum(e, axis=-1, keepdims=True) + jnp.exp(sink - m)
    o = _dot(e.astype(BF16), vx_ref[...]) / den
    for g in range(groups):
        o_ref[:, g * hd:(g + 1) * hd] = o[g * blk:(g + 1) * blk].astype(o_ref.dtype)


def _swa(qkv, sink, hd, batch, seq, ctx_len):
    bw = SWA_HEADS * hd
    groups = SWA_HEADS // SWA_KV_HEADS
    gw = groups * hd
    scale = hd ** -0.5
    blk = SWA_BLOCK
    nb = seq // blk
    kcol = bw // hd
    vcol = kcol + SWA_KV_HEADS
    ctx_row = batch * seq // ctx_len
    smem = pl.BlockSpec(memory_space=pltpu.SMEM)

    qb = min(SWA_Q_BLOCKS, nb)
    ng = nb // qb

    def halo(col, first):
        def idx(b, n, h):
            return (b * nb + jnp.clip(n * qb + first, 0, nb - 1), col + h)
        return pl.BlockSpec((blk, hd), idx)

    own = lambda col: pl.BlockSpec((qb * blk, hd), lambda b, n, h: (b * ng + n, col + h))
    ctx_kv = lambda col: pl.BlockSpec((ctx_len, hd), lambda b, n, h: (ctx_row + b, col + h))
    lat = pl.pallas_call(
        functools.partial(_swa_lat_kernel, seq=seq, groups=groups, hd=hd, scale=scale, blk=blk),
        grid=(batch, ng, SWA_KV_HEADS),
        in_specs=[smem, pl.BlockSpec((qb * blk, gw), lambda b, n, h: (b * ng + n, h)),
                  halo(kcol, -1), own(kcol), halo(kcol, qb), halo(vcol, -1), own(vcol), halo(vcol, qb),
                  ctx_kv(kcol), ctx_kv(vcol)],
        out_specs=pl.BlockSpec((qb * blk, gw), lambda b, n, h: (b * ng + n, h)),
        out_shape=jax.ShapeDtypeStruct((batch * seq, bw), BF16),
        compiler_params=_params("parallel", "parallel", "parallel"),
        name="swa_latent",
    )(sink, qkv, qkv, qkv, qkv, qkv, qkv, qkv, qkv, qkv)
    ctx_kv2 = lambda col: pl.BlockSpec((ctx_len, hd), lambda b, h: (ctx_row + b, col + h))
    ctx = pl.pallas_call(
        functools.partial(_swa_ctx_kernel, groups=groups, hd=hd, scale=scale),
        grid=(batch, SWA_KV_HEADS),
        in_specs=[smem, pl.BlockSpec((ctx_len, gw), lambda b, h: (ctx_row + b, h)),
                  ctx_kv2(kcol), ctx_kv2(vcol)],
        out_specs=pl.BlockSpec((ctx_len, gw), lambda b, h: (b, h)),
        out_shape=jax.ShapeDtypeStruct((batch * ctx_len, bw), BF16),
        compiler_params=_params("parallel", "parallel"),
        name="swa_context",
    )(sink, qkv, qkv, qkv)
    return lat, ctx


def _vt_kernel(v_ref, o_ref):
    heads, vrows, tr = o_ref.shape
    hd2 = v_ref.shape[1] // heads
    for h in range(heads):
        o_ref[h, :hd2, :] = v_ref[:, h * hd2:(h + 1) * hd2].T.astype(o_ref.dtype)
        o_ref[h, hd2:, :] = jnp.ones((vrows - hd2, tr), o_ref.dtype)


def _vt(p, col_block, heads, tr):
    m = p.shape[0]
    hd2 = LANES
    vrows = hd2 + 2 * SUBLANES
    return pl.pallas_call(
        _vt_kernel,
        grid=(m // tr,),
        in_specs=[pl.BlockSpec((tr, heads * hd2), lambda i: (i, col_block))],
        out_specs=pl.BlockSpec((heads, vrows, tr), lambda i: (0, 0, i)),
        out_shape=jax.ShapeDtypeStruct((heads, vrows, m), BF16),
        compiler_params=_params("parallel"),
        name="v_transpose",
    )(p)


def _diff_kernel(dl_ref, g_ref, q_ref, kx_ref, vx_ref, *rest, lam_init, tk, with_latent):
    if with_latent:
        k_ref, v_ref, o_ref, m_ref, acc_ref, s0_ref, s1_ref = rest
    else:
        o_ref, m_ref, acc_ref = rest
    tq, hd2 = q_ref.shape
    hd = hd2 // 2
    qt = q_ref[...].astype(F32).T
    row = lax.broadcasted_iota(jnp.int32, qt.shape, 0)
    qcat = jnp.concatenate([jnp.where(row < hd, qt, 0.0), jnp.where(row < hd, 0.0, qt)], axis=1).astype(BF16)

    def scores(k):
        return _dot(k, qcat)

    def absorb(s, vt):
        m_old = m_ref[...]
        m_new = jnp.maximum(m_old, jnp.max(s, axis=0, keepdims=True))
        alpha = jnp.exp2(m_old - m_new)
        p = jnp.exp2(s - m_new).astype(BF16)
        acc_ref[...] = alpha * acc_ref[...] + _dot(vt, p)
        m_ref[...] = m_new

    m_ref[...] = jnp.full(m_ref.shape, NEG_INF, F32)
    acc_ref[...] = jnp.zeros(acc_ref.shape, F32)
    if with_latent:
        bufs = (s0_ref, s1_ref)
        n_chunks = k_ref.shape[0] // tk
        bufs[0][...] = scores(k_ref[0:tk, :])
        absorb(scores(kx_ref[...]), vx_ref[...])
        for c in range(n_chunks):
            if c + 1 < n_chunks:
                bufs[(c + 1) % 2][...] = scores(k_ref[(c + 1) * tk:(c + 2) * tk, :])
            absorb(bufs[c % 2][...], v_ref[:, c * tk:(c + 1) * tk])
    else:
        absorb(scores(kx_ref[...]), vx_ref[...])
    acc = acc_ref[...]
    on = acc[:hd2] / acc[hd2:hd2 + 1]
    dl = dl_ref[...]
    lam = (jnp.exp(jnp.sum(dl[0:1] * dl[1:2], axis=-1, keepdims=True))
           - jnp.exp(jnp.sum(dl[2:3] * dl[3:4], axis=-1, keepdims=True)) + lam_init)
    o = (on[:, :tq] - lam * on[:, tq:]).T
    y = o * lax.rsqrt(jnp.mean(o * o, axis=-1, keepdims=True) + EPS) * g_ref[...]
    o_ref[...] = (y * (1.0 - lam_init)).astype(o_ref.dtype)


def _diff(qk, vt, diff_lambda, subln, layer, lam_init, batch, seq, ctx_len, tq, tk):
    bw = qk.shape[1] // 2
    hd2 = bw // DIFF_HEADS
    vrows = vt.shape[1]
    flags = dict(lam_init=lam_init, tk=tk)
    kcol = DIFF_HEADS
    ctx_row = batch * seq // ctx_len
    nq = seq // tq
    dl_spec3 = pl.BlockSpec((None,) + diff_lambda.shape[1:], lambda b, h, i: (layer, 0, 0))
    g_spec3 = pl.BlockSpec((None, 1, hd2), lambda b, h, i: (layer, 0, 0))
    lat = pl.pallas_call(
        functools.partial(_diff_kernel, with_latent=True, **flags),
        grid=(batch, DIFF_HEADS, nq),
        in_specs=[dl_spec3, g_spec3,
                  pl.BlockSpec((tq, hd2), lambda b, h, i: (b * nq + i, h)),
                  pl.BlockSpec((ctx_len, hd2), lambda b, h, i: (ctx_row + b, kcol + h)),
                  pl.BlockSpec((None, vrows, ctx_len), lambda b, h, i: (h, 0, ctx_row + b)),
                  pl.BlockSpec((seq, hd2), lambda b, h, i: (b, kcol + h)),
                  pl.BlockSpec((None, vrows, seq), lambda b, h, i: (h, 0, b))],
        out_specs=pl.BlockSpec((tq, hd2), lambda b, h, i: (b * nq + i, h)),
        out_shape=jax.ShapeDtypeStruct((batch * seq, bw), BF16),
        scratch_shapes=[pltpu.VMEM((1, 2 * tq), F32), pltpu.VMEM((vrows, 2 * tq), F32),
                        pltpu.VMEM((tk, 2 * tq), F32), pltpu.VMEM((tk, 2 * tq), F32)],
        compiler_params=_params("parallel", "parallel", "arbitrary"),
        name="diff_latent",
    )(diff_lambda, subln, qk, qk, vt, qk, vt)
    dl_spec2 = pl.BlockSpec((None,) + diff_lambda.shape[1:], lambda b, h: (layer, 0, 0))
    g_spec2 = pl.BlockSpec((None, 1, hd2), lambda b, h: (layer, 0, 0))
    ctx = pl.pallas_call(
        functools.partial(_diff_kernel, with_latent=False, **flags),
        grid=(batch, DIFF_HEADS),
        in_specs=[dl_spec2, g_spec2,
                  pl.BlockSpec((ctx_len, hd2), lambda b, h: (ctx_row + b, h)),
                  pl.BlockSpec((ctx_len, hd2), lambda b, h: (ctx_row + b, kcol + h)),
                  pl.BlockSpec((None, vrows, ctx_len), lambda b, h: (h, 0, ctx_row + b))],
        out_specs=pl.BlockSpec((ctx_len, hd2), lambda b, h: (b, h)),
        out_shape=jax.ShapeDtypeStruct((batch * ctx_len, bw), BF16),
        scratch_shapes=[pltpu.VMEM((1, 2 * ctx_len), F32), pltpu.VMEM((vrows, 2 * ctx_len), F32)],
        compiler_params=_params("parallel", "parallel"),
        name="diff_context",
    )(diff_lambda, subln, qk, qk, vt)
    return lat, ctx


def _f1_kernel(f_ref, u_ref, a_ref):
    l1 = u_ref.shape[0]
    r = _dot(f_ref[...], u_ref[...])
    a_ref[0] = r[:l1].astype(a_ref.dtype)
    a_ref[1] = r[l1:].astype(a_ref.dtype)


def _f2_kernel(g_ref, a_ref, y_ref):
    n = a_ref.shape[1]
    r = _dot(g_ref[:, :n], a_ref[0]) + _dot(g_ref[:, n:], a_ref[1])
    y_ref[0] = r[:n].astype(y_ref.dtype)
    y_ref[1] = r[n:].astype(y_ref.dtype)


def _f3_kernel(yr_ref, yi_ref, c_ref, s_ref, o_ref):
    o_ref[...] = (_dot(yr_ref[...], c_ref[...]) + _dot(yi_ref[...], s_ref[...])).astype(o_ref.dtype)


def _fctx_kernel(f_ref, u_ref, c_ref, s_ref, o_ref):
    n = u_ref.shape[0]
    w = _dot(f_ref[...], u_ref[...])
    wr = w[:n].astype(BF16)
    wi = w[n:].astype(BF16)
    o_ref[...] = (_dot(wr, c_ref[...]) + _dot(wi, s_ref[...])).astype(o_ref.dtype)


def _angle(prod, n):
    return (2.0 * math.pi / n) * (prod % n).astype(F32)


def _dft_pair(n):
    k = jnp.arange(n, dtype=jnp.int32)
    ang = _angle(k[:, None] * k[None, :], n)
    return jnp.concatenate([jnp.cos(ang), -jnp.sin(ang)], axis=0)


def _fourier_tables(seq, ctx_len, gdim):
    l1 = seq // FFT_INNER
    f1 = _dft_pair(l1).astype(BF16)
    k1 = jnp.arange(l1, dtype=jnp.int32)[:, None, None]
    k2 = jnp.arange(FFT_INNER, dtype=jnp.int32)[None, :, None]
    t2 = jnp.arange(FFT_INNER, dtype=jnp.int32)[None, None, :]
    ang = _angle(t2 * (k1 + l1 * k2), seq)
    c, s = jnp.cos(ang), jnp.sin(ang)
    g2 = jnp.concatenate([jnp.concatenate([c, s], axis=2), jnp.concatenate([-s, c], axis=2)], axis=1).astype(BF16)
    kc = jnp.arange(gdim, dtype=jnp.int32)
    angc = _angle(kc[:, None] * kc[None, :], gdim)
    cc, sc = jnp.cos(angc), jnp.sin(angc)
    nl = 1.0 / math.sqrt(seq * gdim)
    nc = 1.0 / math.sqrt(ctx_len * gdim)
    return dict(f1=f1, g2=g2, c3=(cc * nl).astype(BF16), s3=(sc * nl).astype(BF16),
                fc=_dft_pair(ctx_len).astype(BF16), c3c=(cc * nc).astype(BF16), s3c=(sc * nc).astype(BF16))


def _fourier(pf, tabs, batch, seq, ctx_len):
    m, bw = pf.shape
    gdim = bw // FOURIER_GROUPS
    l1 = seq // FFT_INNER
    flat = FFT_INNER * bw
    cc1 = min(flat, 8192)
    a = pl.pallas_call(
        _f1_kernel,
        grid=(batch, flat // cc1),
        in_specs=[pl.BlockSpec((2 * l1, l1), lambda b, j: (0, 0)),
                  pl.BlockSpec((l1, cc1), lambda b, j: (b, j))],
        out_specs=pl.BlockSpec((None, 2, l1, cc1), lambda b, j: (b, 0, 0, j)),
        out_shape=jax.ShapeDtypeStruct((batch, 2, l1, flat), BF16),
        compiler_params=_params("parallel", "parallel"),
        name="fourier_stage1",
    )(tabs["f1"], pf.reshape(m // FFT_INNER, flat))
    y = pl.pallas_call(
        _f2_kernel,
        grid=(batch, l1),
        in_specs=[pl.BlockSpec((None, 2 * FFT_INNER, 2 * FFT_INNER), lambda b, k: (k, 0, 0)),
                  pl.BlockSpec((None, 2, None, FFT_INNER, bw), lambda b, k: (b, 0, k, 0, 0))],
        out_specs=pl.BlockSpec((None, 2, FFT_INNER, bw), lambda b, k: (b, 0, 0, k)),
        out_shape=jax.ShapeDtypeStruct((batch, 2, FFT_INNER, l1 * bw), BF16),
        compiler_params=_params("parallel", "parallel"),
        name="fourier_stage2",
    )(tabs["g2"], a.reshape(batch, 2, l1, FFT_INNER, bw))
    y4 = y.reshape(batch, 2, seq, bw)
    tm3 = min(seq, 1024)
    nt = seq // tm3
    tab = pl.BlockSpec((gdim, gdim), lambda b, i, g: (0, 0))
    lat = pl.pallas_call(
        _f3_kernel,
        grid=(batch, nt, FOURIER_GROUPS),
        in_specs=[pl.BlockSpec((None, None, tm3, gdim), lambda b, i, g: (b, 0, i, g)),
                  pl.BlockSpec((None, None, tm3, gdim), lambda b, i, g: (b, 1, i, g)), tab, tab],
        out_specs=pl.BlockSpec((tm3, gdim), lambda b, i, g: (b * nt + i, g)),
        out_shape=jax.ShapeDtypeStruct((batch * seq, bw), BF16),
        compiler_params=_params("parallel", "parallel", "parallel"),
        name="fourier_stage3",
    )(y4, y4, tabs["c3"], tabs["s3"])
    ctx_row = batch * seq // ctx_len
    tab2 = pl.BlockSpec((gdim, gdim), lambda b, g: (0, 0))
    ctx = pl.pallas_call(
        _fctx_kernel,
        grid=(batch, FOURIER_GROUPS),
        in_specs=[pl.BlockSpec((2 * ctx_len, ctx_len), lambda b, g: (0, 0)),
                  pl.BlockSpec((ctx_len, gdim), lambda b, g: (ctx_row + b, g)), tab2, tab2],
        out_specs=pl.BlockSpec((ctx_len, gdim), lambda b, g: (b, g)),
        out_shape=jax.ShapeDtypeStruct((batch * ctx_len, bw), BF16),
        compiler_params=_params("parallel", "parallel"),
        name="fourier_context",
    )(tabs["fc"], pf, tabs["c3c"], tabs["s3c"])
    return lat, ctx


def _pick(n, prefs):
    for t in prefs:
        if n % t == 0:
            return t
    return n


def kernel(x, c, ctx, c_ctx, norm_g, ada_down, ada_up, ada_b, ffn_w13, ffn_w2, w_in, conv_w, swa_q_norm,
           swa_k_norm, swa_sink, diff_q_norm, diff_k_norm, diff_lambda, diff_subln, w_branch, gate_up, w_out):
    batch, seq, d = x.shape
    ctx_len = ctx.shape[1]
    depth = norm_g.shape[0]
    bw = d // 4
    rank = gate_up.shape[1]
    d_ff = ffn_w2.shape[2]
    n_lat = batch * seq
    n_ctx = batch * ctx_len
    m = n_lat + n_ctx
    tm, tr = ROW_TILE, ELT_ROW_TILE
    swa_hd = bw // SWA_HEADS
    diff_hd = bw // (2 * DIFF_HEADS)
    assert swa_hd == LANES and 2 * diff_hd == LANES
    assert seq % tm == 0 and n_ctx % tm == 0 and seq % FFT_INNER == 0 and seq % GRID_W == 0
    assert seq & (seq - 1) == 0 and ctx_len & (ctx_len - 1) == 0 and n_lat % ctx_len == 0
    assert w_in.shape[2] == 8 * bw + 2 * SWA_KV_HEADS * swa_hd + rank

    def mod_row_for(layer):
        def mod_row(i, tile):
            who = jnp.minimum(i // (seq // tile), batch)
            return (layer * (batch + 1) + who) * N_MOD
        return mod_row

    def rope_row(i, tile):
        return jnp.where(i < n_lat // tile, i % (seq // tile), seq // tile)

    cv = jnp.concatenate([c, c_ctx[None, :], jnp.zeros((SUBLANES - batch - 1, d), F32)], axis=0)
    mods = _ada(cv, ada_down, ada_up, ada_b)[:, :batch + 1]
    mods = mods.reshape(depth * (batch + 1) * N_MOD, 1, d)
    gains = norm_g.reshape(depth * 3, 1, d)

    w13 = ffn_w13.astype(BF16)
    w2 = ffn_w2.astype(BF16)
    w_branch_b = w_branch.astype(BF16)
    gate_up_b = gate_up.astype(BF16).reshape(depth, rank, -1)
    w_out_b = w_out.astype(BF16)
    kv_w = SWA_KV_HEADS * swa_hd
    sw_w = bw + 2 * kv_w
    cuts = (0, bw, 4 * bw, 4 * bw + sw_w, 7 * bw + sw_w, 7 * bw + sw_w + rank)
    w_f, w_cv, w_sw, w_df, w_z = (w_in[:, :, a:b].astype(BF16) for a, b in zip(cuts[:-1], cuts[1:]))

    cos_s, sin_s = _rope_tables(seq, swa_hd, tr)
    cos_d, sin_d = _rope_tables(seq, diff_hd, tr)
    tabs = _fourier_tables(seq, ctx_len, bw // FOURIER_GROUPS)

    tn_ff = _pick(d_ff, COL_TILES)
    tn_d = _pick(d, COL_TILES)
    tn_bw = _pick(bw, COL_TILES)
    tn_sw = _pick(sw_w, (768,) + COL_TILES)
    swa_gains = jnp.concatenate([jnp.tile(swa_q_norm, (1, SWA_HEADS)), jnp.tile(swa_k_norm, (1, SWA_KV_HEADS)),
                                 jnp.ones((depth, kv_w), F32)], axis=1).reshape(depth, 1, sw_w)
    diff_gains = jnp.concatenate([jnp.tile(diff_q_norm, (1, 2 * DIFF_HEADS)),
                                  jnp.tile(diff_k_norm, (1, 2 * DIFF_HEADS))], axis=1).reshape(depth, 1, 2 * bw)
    subln = diff_subln.reshape(depth, 1, 2 * diff_hd)

    t = jnp.concatenate([x.reshape(n_lat, d), ctx.reshape(n_ctx, d)], axis=0)

    def ffn_half(t, layer, sub, k0, rows):
        mod_row = mod_row_for(layer)
        h = _normmod(t, gains, layer * 3 + 2 * sub, mods, mod_row, k0, k0 + 1, tr)
        act = _mm_swiglu(h, w13, (layer, sub), tn_ff, tm)
        return _mm_resid(act, w2, (layer, sub), t, mods, mod_row, k0 + 2, 0.5, tn_d, tm, rows)

    for layer in range(depth):
        last = layer == depth - 1
        rows = n_lat if last else m
        mod_row = mod_row_for(layer)
        lam_init = 0.8 - 0.6 * math.exp(-0.3 * layer)
        t = ffn_half(t, layer, 0, 0, m)
        h = _normmod(t, gains, layer * 3 + 1, mods, mod_row, 3, 4, tr)
        pf = _mm(h, w_f, (layer,), tn_bw, BF16, tm)
        pcv = _mm(h, w_cv, (layer,), tn_bw, F32, tm)
        psw = _mm(h, w_sw, (layer,), tn_sw, F32, tm)
        pdf = _mm(h, w_df, (layer,), tn_bw, F32, tm)
        pz = _mm(h, w_z, (layer,), rank, BF16, tm)

        o_f = _fourier(pf, tabs, batch, seq, ctx_len) + (0,)
        o_c = _conv(pcv, conv_w, layer, seq, ctx_len, n_lat, tr)
        qkv_s = _prep(psw, sw_w, swa_gains[layer], cos_s, sin_s, swa_hd, swa_hd // 4,
                      SWA_HEADS + SWA_KV_HEADS, 0, 1.0, rope_row, tr)
        o_s = _swa(qkv_s, swa_sink[layer], swa_hd, batch, seq, ctx_len) + (0,)
        qk_d = _prep(pdf, 2 * bw, diff_gains[layer], cos_d, sin_d, diff_hd, diff_hd // 4,
                     2 * DIFF_HEADS, DIFF_HEADS, diff_hd ** -0.5 * LOG2_E, rope_row, tr)
        vt_d = _vt(pdf, 2, DIFF_HEADS, tr)
        o_d = _diff(qk_d, vt_d, diff_lambda, subln, layer, lam_init, batch, seq, ctx_len,
                    min(seq, DIFF_Q_TILE), min(seq, DIFF_K_TILE)) + (0,)

        merged = _merge([o_f, (o_c, o_c, n_lat // tm), o_s, o_d], pz, w_branch_b, gate_up_b, layer,
                        tn_d, tm, rows, n_lat)
        t = _mm_resid(merged, w_out_b, (layer,), t, mods, mod_row, 5, 1.0, tn_d, tm, rows)
        t = ffn_half(t, layer, 1, 6, rows)
    return t.reshape(batch, seq, d)
```

```python
import functools
import math

import jax
import jax.numpy as jnp
from jax import lax
from jax.experimental import pallas as pl
from jax.experimental.pallas import tpu as pltpu

F32 = jnp.float32
BF16 = jnp.bfloat16

EPS = 1e-6
NEG_INF = -1e30
ROPE_BASE = 10000.0
GRID_W = 64
N_MOD = 9
FOURIER_GROUPS = 4
SWA_HEADS = 8
SWA_KV_HEADS = 2
SWA_BLOCK = 128
SWA_WINDOW = 128
DIFF_HEADS = 8
LANES = 128
SUBLANES = 8
FFT_INNER = 128
FFT_K1_BLOCK = 4
FFT_STAGE1_COLS = 8192
FFT_STAGE3_ROWS = 2048

VMEM_LIMIT_BYTES = 56 * 1024 * 1024
ROW_TILE = 512
FREE_ROW_TILE = 768
ELT_ROW_TILE = 512
DIFF_Q_TILE = 1024
DIFF_K_TILE = 1024
DIFF_SCORE_BUFFERS = 2
LOG2_E = 1.4426950408889634
COL_TILES = (1024, 512, 256, 128)
SWA_COL_TILE = 768
SWA_Q_BLOCKS = 4

_HIGHEST = lax.Precision.HIGHEST
_NT_DIMS = (((1,), (1,)), ((), ()))


def _params(*sem):
    return pltpu.CompilerParams(dimension_semantics=sem, vmem_limit_bytes=VMEM_LIMIT_BYTES)


def _dot(a, b):
    return jnp.dot(a, b, preferred_element_type=F32)


def _ada_kernel(cv_ref, down_ref, up_ref, b_ref, o_ref, hid_ref):
    @pl.when(pl.program_id(1) == 0)
    def _():
        c = cv_ref[...]
        s = c * jax.nn.sigmoid(c)
        hid_ref[...] = jnp.dot(s, down_ref[...], precision=_HIGHEST, preferred_element_type=F32)

    o_ref[...] = jnp.dot(hid_ref[...], up_ref[...], precision=_HIGHEST,
                         preferred_element_type=F32) + b_ref[...]


def _ada(cv, down, up, b):
    depth, d, r = down.shape
    n = up.shape[2]
    tn = n // N_MOD
    return pl.pallas_call(
        _ada_kernel,
        grid=(depth, n // tn),
        in_specs=[pl.BlockSpec((SUBLANES, d), lambda l, j: (0, 0)),
                  pl.BlockSpec((None, d, r), lambda l, j: (l, 0, 0)),
                  pl.BlockSpec((None, r, tn), lambda l, j: (l, 0, j)),
                  pl.BlockSpec((None, 1, tn), lambda l, j: (l, 0, j))],
        out_specs=pl.BlockSpec((None, SUBLANES, tn), lambda l, j: (l, 0, j)),
        out_shape=jax.ShapeDtypeStruct((depth, SUBLANES, n), F32),
        scratch_shapes=[pltpu.VMEM((SUBLANES, r), F32)],
        compiler_params=_params("parallel", "arbitrary"),
        name="ada_modulation",
    )(cv, down, up, b.reshape(depth, 1, n))


def _normmod_kernel(t_ref, g_ref, sh_ref, sc_ref, o_ref):
    x = t_ref[...]
    ms = jnp.mean(x * x, axis=-1, keepdims=True)
    y = x * lax.rsqrt(ms + EPS) * g_ref[...]
    o_ref[...] = (y * (1.0 + sc_ref[...]) + sh_ref[...]).astype(o_ref.dtype)


def _normmod(t, gains, g_row, mods, mod_row, k_shift, k_scale, tr):
    m, d = t.shape
    vec = lambda idx: pl.BlockSpec((None, 1, d), idx)
    return pl.pallas_call(
        _normmod_kernel,
        grid=(m // tr,),
        in_specs=[pl.BlockSpec((tr, d), lambda i: (i, 0)),
                  vec(lambda i: (g_row, 0, 0)),
                  vec(lambda i: (mod_row(i, tr) + k_shift, 0, 0)),
                  vec(lambda i: (mod_row(i, tr) + k_scale, 0, 0))],
        out_specs=pl.BlockSpec((tr, d), lambda i: (i, 0)),
        out_shape=jax.ShapeDtypeStruct((m, d), BF16),
        compiler_params=_params("parallel"),
        name="norm_modulate",
    )(t, gains, mods, mods)


def _mm_kernel(x_ref, w_ref, o_ref):
    o_ref[...] = _dot(x_ref[...], w_ref[...]).astype(o_ref.dtype)


def _w_spec(widx, k, tn, col_block):
    lead = (None,) * len(widx)
    return pl.BlockSpec(lead + (k, tn), lambda j, i: widx + (0, col_block + j))


def _mm(x, w, widx, tn, out_dtype, tm):
    m, k = x.shape
    ncols = w.shape[-1]
    return pl.pallas_call(
        _mm_kernel,
        grid=(ncols // tn, m // tm),
        in_specs=[pl.BlockSpec((tm, k), lambda j, i: (i, 0)),
                  _w_spec(widx, k, tn, 0)],
        out_specs=pl.BlockSpec((tm, tn), lambda j, i: (i, j)),
        out_shape=jax.ShapeDtypeStruct((m, ncols), out_dtype),
        compiler_params=_params("parallel", "parallel"),
        name="matmul",
    )(x, w)


def _mm_swiglu_kernel(x_ref, wa_ref, wu_ref, o_ref):
    x = x_ref[...]
    a = _dot(x, wa_ref[...])
    u = _dot(x, wu_ref[...])
    o_ref[...] = (a * jax.nn.sigmoid(a) * u).astype(o_ref.dtype)


def _mm_swiglu(x, w13, widx, tn, tm):
    m, k = x.shape
    f = w13.shape[-1] // 2
    return pl.pallas_call(
        _mm_swiglu_kernel,
        grid=(f // tn, m // tm),
        in_specs=[pl.BlockSpec((tm, k), lambda j, i: (i, 0)),
                  _w_spec(widx, k, tn, 0),
                  _w_spec(widx, k, tn, f // tn)],
        out_specs=pl.BlockSpec((tm, tn), lambda j, i: (i, j)),
        out_shape=jax.ShapeDtypeStruct((m, f), BF16),
        compiler_params=_params("parallel", "parallel"),
        name="matmul_swiglu",
    )(x, w13, w13)


def _mm_resid_kernel(x_ref, w_ref, t_ref, g_ref, o_ref, *, coef):
    y = _dot(x_ref[...], w_ref[...])
    o_ref[...] = t_ref[...] + (coef * g_ref[...]) * y


def _mm_resid(x, w, widx, t, mods, mod_row, k_gate, coef, tn, tm, m_rows):
    k = x.shape[1]
    d = t.shape[1]
    return pl.pallas_call(
        functools.partial(_mm_resid_kernel, coef=coef),
        grid=(d // tn, m_rows // tm),
        in_specs=[pl.BlockSpec((tm, k), lambda j, i: (i, 0)),
                  _w_spec(widx, k, tn, 0),
                  pl.BlockSpec((tm, tn), lambda j, i: (i, j)),
                  pl.BlockSpec((None, 1, tn), lambda j, i: (mod_row(i, tm) + k_gate, 0, j))],
        out_specs=pl.BlockSpec((tm, tn), lambda j, i: (i, j)),
        out_shape=jax.ShapeDtypeStruct((m_rows, d), F32),
        compiler_params=_params("parallel", "parallel"),
        name="matmul_gated_residual",
    )(x, w, t, mods)


def _merge_kernel(*refs, nbr, n_lat_tiles):
    lat_refs, ctx_refs = refs[:nbr], refs[nbr:2 * nbr]
    z_ref, wb_ref = refs[2 * nbr:2 * nbr + 2]
    g_refs = refs[2 * nbr + 2:3 * nbr + 2]
    out_ref = refs[-1]

    def compute(o_refs):
        z = z_ref[...]
        acc = None
        for b in range(nbr):
            term = jax.nn.sigmoid(_dot(z, g_refs[b][...])) * _dot(o_refs[b][...], wb_ref[b])
            acc = term if acc is None else acc + term
        out_ref[...] = acc.astype(out_ref.dtype)

    is_lat = pl.program_id(1) < n_lat_tiles
    pl.when(is_lat)(lambda: compute(lat_refs))
    pl.when(jnp.logical_not(is_lat))(lambda: compute(ctx_refs))


def _merge(branches, z, w_branch, gate_up2, layer, tn, tm, m_rows, n_lat):
    bw = branches[0][0].shape[1]
    r = z.shape[1]
    nbr, d = w_branch.shape[1], w_branch.shape[3]
    nl = n_lat // tm
    lat = pl.BlockSpec((tm, bw), lambda j, i: (jnp.minimum(i, nl - 1), 0))
    ctx = lambda first: pl.BlockSpec((tm, bw), lambda j, i: (jnp.maximum(i - nl, 0) + first, 0))
    gate = lambda b: pl.BlockSpec((None, r, tn), lambda j, i: (layer, 0, b * (d // tn) + j))
    return pl.pallas_call(
        functools.partial(_merge_kernel, nbr=nbr, n_lat_tiles=nl),
        grid=(d // tn, m_rows // tm),
        in_specs=[lat] * nbr + [ctx(first) for _, _, first in branches]
                 + [pl.BlockSpec((tm, r), lambda j, i: (i, 0)),
                    pl.BlockSpec((None, nbr, bw, tn), lambda j, i: (layer, 0, 0, j))]
                 + [gate(b) for b in range(nbr)],
        out_specs=pl.BlockSpec((tm, tn), lambda j, i: (i, j)),
        out_shape=jax.ShapeDtypeStruct((m_rows, d), BF16),
        compiler_params=_params("parallel", "parallel"),
        name="gated_merge",
    )(*[a for a, _, _ in branches], *[c for _, c, _ in branches], z, w_branch, *([gate_up2] * nbr))


def _prep_kernel(x_ref, g_ref, cos_ref, sin_ref, o_ref, *, seg, half, n_rope, n_scaled, qscale):
    cos = cos_ref[...]
    sin = sin_ref[...]
    ri = lax.broadcasted_iota(jnp.int32, (2 * LANES, LANES), 0) & (LANES - 1)
    ci = lax.broadcasted_iota(jnp.int32, (2 * LANES, LANES), 1)
    seg_ind = jnp.where((ri & -seg) == (ci & -seg), 1.0, 0.0).astype(BF16)
    for c in range(o_ref.shape[1] // LANES):
        sl = slice(c * LANES, (c + 1) * LANES)
        x = x_ref[:, sl]
        if c >= n_rope:
            o_ref[:, sl] = x.astype(o_ref.dtype)
            continue
        lane = lax.broadcasted_iota(jnp.int32, x.shape, 1)
        ss = x * x
        ss_hi = ss.astype(BF16)
        ss_lo = (ss - ss_hi.astype(F32)).astype(BF16)
        ms = _dot(jnp.concatenate([ss_hi, ss_lo], axis=1), seg_ind) * (1.0 / seg)
        xn = x * lax.rsqrt(ms + EPS) * g_ref[:, sl]
        lower = pltpu.roll(xn, half, 1)
        upper = pltpu.roll(xn, LANES - half, 1)
        partner = jnp.where((lane & half) == 0, upper, lower)
        y = xn * cos + partner * sin
        if c < n_scaled:
            y = y * qscale
        o_ref[:, sl] = y.astype(o_ref.dtype)


def _prep(p, n, gains, cos_t, sin_t, seg, half, n_rope, n_scaled, qscale, rope_row, tr):
    m = p.shape[0]
    return pl.pallas_call(
        functools.partial(_prep_kernel, seg=seg, half=half, n_rope=n_rope, n_scaled=n_scaled, qscale=qscale),
        grid=(m // tr,),
        in_specs=[pl.BlockSpec((tr, n), lambda i: (i, 0)),
                  pl.BlockSpec((1, n), lambda i: (0, 0)),
                  pl.BlockSpec((tr, LANES), lambda i: (rope_row(i, tr), 0)),
                  pl.BlockSpec((tr, LANES), lambda i: (rope_row(i, tr), 0))],
        out_specs=pl.BlockSpec((tr, n), lambda i: (i, 0)),
        out_shape=jax.ShapeDtypeStruct((m, n), BF16),
        compiler_params=_params("parallel"),
        name="qk_norm_rope",
    )(p, gains, cos_t, sin_t)


def _rope_tables(seq, head_dim, ident_rows):
    rows = seq // GRID_W
    row = jnp.repeat(jnp.arange(rows, dtype=jnp.int32), GRID_W)
    col = jnp.tile(jnp.arange(GRID_W, dtype=jnp.int32), rows)
    axis_dim = head_dim // 2
    inv_freq = ROPE_BASE ** (-jnp.arange(0, axis_dim, 2, dtype=F32) / axis_dim)
    ar = row.astype(F32)[:, None] * inv_freq
    ac = col.astype(F32)[:, None] * inv_freq
    cr, sr, cc, sc = jnp.cos(ar), jnp.sin(ar), jnp.cos(ac), jnp.sin(ac)
    cos_t = jnp.concatenate([cr, cr, cc, cc], axis=-1)
    sin_t = jnp.concatenate([-sr, sr, -sc, sc], axis=-1)
    reps = LANES // head_dim
    cos_t = jnp.tile(cos_t, (1, reps))
    sin_t = jnp.tile(sin_t, (1, reps))
    cos_t = jnp.concatenate([cos_t, jnp.ones((ident_rows, LANES), F32)], axis=0)
    sin_t = jnp.concatenate([sin_t, jnp.zeros((ident_rows, LANES), F32)], axis=0)
    return cos_t, sin_t


def _conv_kernel(gb_ref, gc_ref, v_ref, gcp_ref, vp_ref, gcn_ref, vn_ref, w_ref, o_ref, *, seq, ctx_len, n_lat):
    tr = gb_ref.shape[0]
    i = pl.program_id(0)
    u = gc_ref[...] * v_ref[...]
    u_before = (gcp_ref[...] * vp_ref[...])[SUBLANES - 1:SUBLANES, :]
    u_after = (gcn_ref[...] * vn_ref[...])[0:1, :]
    r = lax.broadcasted_iota(jnp.int32, (tr, 1), 0)
    gr = i * tr + r
    is_lat = gr < n_lat
    pos = jnp.where(is_lat, gr & (seq - 1), (gr - n_lat) & (ctx_len - 1))
    last = jnp.where(is_lat, seq - 1, ctx_len - 1)
    u_m = jnp.where(r == 0, u_before, pltpu.roll(u, 1, 0))
    u_m = jnp.where(pos == 0, 0.0, u_m)
    u_p = jnp.where(r == tr - 1, u_after, pltpu.roll(u, tr - 1, 0))
    u_p = jnp.where(pos == last, 0.0, u_p)
    y = u_m * w_ref[0:1, :] + u * w_ref[1:2, :] + u_p * w_ref[2:3, :]
    o_ref[...] = (gb_ref[...] * y).astype(o_ref.dtype)


def _conv(pcv, conv_w, layer, seq, ctx_len, n_lat, tr):
    m = pcv.shape[0]
    bw = pcv.shape[1] // 3
    halo = tr // SUBLANES
    n_halo = m // SUBLANES
    main = lambda c: pl.BlockSpec((tr, bw), lambda i: (i, c))
    before = lambda c: pl.BlockSpec((SUBLANES, bw), lambda i: (jnp.maximum(i * halo - 1, 0), c))
    after = lambda c: pl.BlockSpec((SUBLANES, bw), lambda i: (jnp.minimum((i + 1) * halo, n_halo - 1), c))
    return pl.pallas_call(
        functools.partial(_conv_kernel, seq=seq, ctx_len=ctx_len, n_lat=n_lat),
        grid=(m // tr,),
        in_specs=[main(0), main(1), main(2), before(1), before(2), after(1), after(2),
                  pl.BlockSpec((None, 3, bw), lambda i: (layer, 0, 0))],
        out_specs=pl.BlockSpec((tr, bw), lambda i: (i, 0)),
        out_shape=jax.ShapeDtypeStruct((m, bw), BF16),
        compiler_params=_params("parallel"),
        name="short_conv",
    )(pcv, pcv, pcv, pcv, pcv, pcv, pcv, conv_w)


def _sink_column(sink_ref, h, groups, blk):
    r = lax.broadcasted_iota(jnp.int32, (groups * blk, 1), 0)
    col = jnp.zeros((groups * blk, 1), F32)
    for g in range(groups):
        col = jnp.where((r >= g * blk) & (r < (g + 1) * blk), sink_ref[h * groups + g], col)
    return col


def _stack_heads(q, groups, hd):
    return jnp.concatenate([q[:, g * hd:(g + 1) * hd] for g in range(groups)], axis=0)


def _swa_lat_kernel(sink_ref, q_ref, kp_ref, kc_ref, kn_ref, vp_ref, vc_ref, vn_ref, kx_ref, vx_ref, o_ref,
                    *, seq, groups, hd, scale, blk):
    n0 = pl.program_id(1) * (q_ref.shape[0] // blk)
    h = pl.program_id(2)
    k_all = jnp.concatenate([kp_ref[...], kc_ref[...], kn_ref[...]], axis=0)
    v_all = jnp.concatenate([vp_ref[...], vc_ref[...], vn_ref[...]], axis=0)
    kx = kx_ref[...]
    vx = vx_ref[...]
    sink = _sink_column(sink_ref, h, groups, blk)
    ri = lax.broadcasted_iota(jnp.int32, (groups * blk, 3 * blk), 0)
    ci = lax.broadcasted_iota(jnp.int32, (groups * blk, 3 * blk), 1)
    in_window = jnp.abs(ci - blk - (ri & (blk - 1))) <= SWA_WINDOW
    for qb in range(q_ref.shape[0] // blk):
        rows = slice(qb * blk, (qb + 1) * blk)
        q = _stack_heads(q_ref[rows, :], groups, hd)
        kw = k_all[qb * blk:(qb + 3) * blk]
        vw = v_all[qb * blk:(qb + 3) * blk]
        s_loc = lax.dot_general(q, kw, _NT_DIMS, preferred_element_type=F32) * scale
        s_ctx = lax.dot_general(q, kx, _NT_DIMS, preferred_element_type=F32) * scale
        kpos = (n0 + qb - 1) * blk + ci
        valid = in_window & (kpos >= 0) & (kpos < seq)
        s_loc = jnp.where(valid, s_loc, NEG_INF)
        m = jnp.maximum(jnp.maximum(jnp.max(s_ctx, axis=-1, keepdims=True),
                                    jnp.max(s_loc, axis=-1, keepdims=True)), sink)
        e_ctx = jnp.exp(s_ctx - m)
        e_loc = jnp.exp(s_loc - m)
        den = (jnp.sum(e_ctx, axis=-1, keepdims=True) + jnp.sum(e_loc, axis=-1, keepdims=True)
               + jnp.exp(sink - m))
        o = (_dot(e_ctx.astype(BF16), vx) + _dot(e_loc.astype(BF16), vw)) / den
        for g in range(groups):
            o_ref[rows, g * hd:(g + 1) * hd] = o[g * blk:(g + 1) * blk].astype(o_ref.dtype)


def _swa_ctx_kernel(sink_ref, q_ref, kx_ref, vx_ref, o_ref, *, groups, hd, scale):
    h = pl.program_id(1)
    blk = q_ref.shape[0]
    q = _stack_heads(q_ref[...], groups, hd)
    s = lax.dot_general(q, kx_ref[...], _NT_DIMS, preferred_element_type=F32) * scale
    sink = _sink_column(sink_ref, h, groups, blk)
    m = jnp.maximum(jnp.max(s, axis=-1, keepdims=True), sink)
    e = jnp.exp(s - m)
    den = jnp.sum(e, axis=-1, keepdims=True) + jnp.exp(sink - m)
    o = _dot(e.astype(BF16), vx_ref[...]) / den
    for g in range(groups):
        o_ref[:, g * hd:(g + 1) * hd] = o[g * blk:(g + 1) * blk].astype(o_ref.dtype)


def _swa(qkv, sink, hd, batch, seq, ctx_len):
    bw = SWA_HEADS * hd
    groups = SWA_HEADS // SWA_KV_HEADS
    gw = groups * hd
    scale = hd ** -0.5
    blk = SWA_BLOCK
    nb = seq // blk
    kcol = bw // hd
    vcol = kcol + SWA_KV_HEADS
    ctx_row = batch * seq // ctx_len
    smem = pl.BlockSpec(memory_space=pltpu.SMEM)

    qb = min(SWA_Q_BLOCKS, nb)
    ng = nb // qb

    def halo(col, first):
        def idx(b, n, h):
            return (b * nb + jnp.clip(n * qb + first, 0, nb - 1), col + h)
        return pl.BlockSpec((blk, hd), idx)

    own = lambda col: pl.BlockSpec((qb * blk, hd), lambda b, n, h: (b * ng + n, col + h))
    ctx_kv = lambda col: pl.BlockSpec((ctx_len, hd), lambda b, n, h: (ctx_row + b, col + h))
    lat = pl.pallas_call(
        functools.partial(_swa_lat_kernel, seq=seq, groups=groups, hd=hd, scale=scale, blk=blk),
        grid=(batch, ng, SWA_KV_HEADS),
        in_specs=[smem, pl.BlockSpec((qb * blk, gw), lambda b, n, h: (b * ng + n, h)),
                  halo(kcol, -1), own(kcol), halo(kcol, qb), halo(vcol, -1), own(vcol), halo(vcol, qb),
                  ctx_kv(kcol), ctx_kv(vcol)],
        out_specs=pl.BlockSpec((qb * blk, gw), lambda b, n, h: (b * ng + n, h)),
        out_shape=jax.ShapeDtypeStruct((batch * seq, bw), BF16),
        compiler_params=_params("parallel", "parallel", "parallel"),
        name="swa_latent",
    )(sink, qkv, qkv, qkv, qkv, qkv, qkv, qkv, qkv, qkv)
    ctx_kv2 = lambda col: pl.BlockSpec((ctx_len, hd), lambda b, h: (ctx_row + b, col + h))
    ctx = pl.pallas_call(
        functools.partial(_swa_ctx_kernel, groups=groups, hd=hd, scale=scale),
        grid=(batch, SWA_KV_HEADS),
        in_specs=[smem, pl.BlockSpec((ctx_len, gw), lambda b, h: (ctx_row + b, h)),
                  ctx_kv2(kcol), ctx_kv2(vcol)],
        out_specs=pl.BlockSpec((ctx_len, gw), lambda b, h: (b, h)),
        out_shape=jax.ShapeDtypeStruct((batch * ctx_len, bw), BF16),
        compiler_params=_params("parallel", "parallel"),
        name="swa_context",
    )(sink, qkv, qkv, qkv)
    return lat, ctx


def _vt_kernel(v_ref, o_ref):
    heads, vrows, tr = o_ref.shape
    hd2 = v_ref.shape[1] // heads
    for h in range(heads):
        o_ref[h, :hd2, :] = v_ref[:, h * hd2:(h + 1) * hd2].T.astype(o_ref.dtype)
        o_ref[h, hd2:, :] = jnp.ones((vrows - hd2, tr), o_ref.dtype)


def _vt(p, col_block, heads, tr):
    m = p.shape[0]
    hd2 = LANES
    vrows = hd2 + 2 * SUBLANES
    return pl.pallas_call(
        _vt_kernel,
        grid=(m // tr,),
        in_specs=[pl.BlockSpec((tr, heads * hd2), lambda i: (i, col_block))],
        out_specs=pl.BlockSpec((heads, vrows, tr), lambda i: (0, 0, i)),
        out_shape=jax.ShapeDtypeStruct((heads, vrows, m), BF16),
        compiler_params=_params("parallel"),
        name="v_transpose",
    )(p)


def _diff_kernel(dl_ref, g_ref, q_ref, kx_ref, vx_ref, *rest, lam_init, tk, with_latent):
    if with_latent:
        k_ref, v_ref, o_ref, m_ref, acc_ref, *bufs = rest
    else:
        o_ref, m_ref, acc_ref = rest
    tq, hd2 = q_ref.shape
    hd = hd2 // 2
    qt = q_ref[...].astype(F32).T
    row = lax.broadcasted_iota(jnp.int32, qt.shape, 0)
    qcat = jnp.concatenate([jnp.where(row < hd, qt, 0.0), jnp.where(row < hd, 0.0, qt)], axis=1).astype(BF16)

    def scores(k):
        return _dot(k, qcat)

    def absorb(s, vt):
        m_old = m_ref[...]
        m_new = jnp.maximum(m_old, jnp.max(s, axis=0, keepdims=True))
        alpha = jnp.exp2(m_old - m_new)
        p = jnp.exp2(s - m_new).astype(BF16)
        acc_ref[...] = alpha * acc_ref[...] + _dot(vt, p)
        m_ref[...] = m_new

    m_ref[...] = jnp.full(m_ref.shape, NEG_INF, F32)
    acc_ref[...] = jnp.zeros(acc_ref.shape, F32)
    if with_latent:
        n_chunks = k_ref.shape[0] // tk
        ahead = len(bufs) - 1

        def issue_scores(c):
            bufs[c % len(bufs)][...] = scores(k_ref[c * tk:(c + 1) * tk, :])

        for c in range(min(ahead, n_chunks)):
            issue_scores(c)
        absorb(scores(kx_ref[...]), vx_ref[...])
        for c in range(n_chunks):
            if c + ahead < n_chunks:
                issue_scores(c + ahead)
            absorb(bufs[c % len(bufs)][...], v_ref[:, c * tk:(c + 1) * tk])
    else:
        absorb(scores(kx_ref[...]), vx_ref[...])
    acc = acc_ref[...]
    on = acc[:hd2] / acc[hd2:hd2 + 1]
    dl = dl_ref[...]
    lam = (jnp.exp(jnp.sum(dl[0:1] * dl[1:2], axis=-1, keepdims=True))
           - jnp.exp(jnp.sum(dl[2:3] * dl[3:4], axis=-1, keepdims=True)) + lam_init)
    o = (on[:, :tq] - lam * on[:, tq:]).T
    y = o * lax.rsqrt(jnp.mean(o * o, axis=-1, keepdims=True) + EPS) * g_ref[...]
    o_ref[...] = (y * (1.0 - lam_init)).astype(o_ref.dtype)


def _diff(qk, vt, diff_lambda, subln, layer, lam_init, batch, seq, ctx_len, tq, tk):
    bw = qk.shape[1] // 2
    hd2 = bw // DIFF_HEADS
    vrows = vt.shape[1]
    flags = dict(lam_init=lam_init, tk=tk)
    kcol = DIFF_HEADS
    ctx_row = batch * seq // ctx_len
    nq = seq // tq
    dl_spec3 = pl.BlockSpec((None,) + diff_lambda.shape[1:], lambda b, h, i: (layer, 0, 0))
    g_spec3 = pl.BlockSpec((None, 1, hd2), lambda b, h, i: (layer, 0, 0))
    lat = pl.pallas_call(
        functools.partial(_diff_kernel, with_latent=True, **flags),
        grid=(batch, DIFF_HEADS, nq),
        in_specs=[dl_spec3, g_spec3,
                  pl.BlockSpec((tq, hd2), lambda b, h, i: (b * nq + i, h)),
                  pl.BlockSpec((ctx_len, hd2), lambda b, h, i: (ctx_row + b, kcol + h)),
                  pl.BlockSpec((None, vrows, ctx_len), lambda b, h, i: (h, 0, ctx_row + b)),
                  pl.BlockSpec((seq, hd2), lambda b, h, i: (b, kcol + h)),
                  pl.BlockSpec((None, vrows, seq), lambda b, h, i: (h, 0, b))],
        out_specs=pl.BlockSpec((tq, hd2), lambda b, h, i: (b * nq + i, h)),
        out_shape=jax.ShapeDtypeStruct((batch * seq, bw), BF16),
        scratch_shapes=[pltpu.VMEM((1, 2 * tq), F32), pltpu.VMEM((vrows, 2 * tq), F32),
                        ] + [pltpu.VMEM((tk, 2 * tq), F32)] * DIFF_SCORE_BUFFERS,
        compiler_params=_params("parallel", "parallel", "arbitrary"),
        name="diff_latent",
    )(diff_lambda, subln, qk, qk, vt, qk, vt)
    dl_spec2 = pl.BlockSpec((None,) + diff_lambda.shape[1:], lambda b, h: (layer, 0, 0))
    g_spec2 = pl.BlockSpec((None, 1, hd2), lambda b, h: (layer, 0, 0))
    ctx = pl.pallas_call(
        functools.partial(_diff_kernel, with_latent=False, **flags),
        grid=(batch, DIFF_HEADS),
        in_specs=[dl_spec2, g_spec2,
                  pl.BlockSpec((ctx_len, hd2), lambda b, h: (ctx_row + b, h)),
                  pl.BlockSpec((ctx_len, hd2), lambda b, h: (ctx_row + b, kcol + h)),
                  pl.BlockSpec((None, vrows, ctx_len), lambda b, h: (h, 0, ctx_row + b))],
        out_specs=pl.BlockSpec((ctx_len, hd2), lambda b, h: (b, h)),
        out_shape=jax.ShapeDtypeStruct((batch * ctx_len, bw), BF16),
        scratch_shapes=[pltpu.VMEM((1, 2 * ctx_len), F32), pltpu.VMEM((vrows, 2 * ctx_len), F32)],
        compiler_params=_params("parallel", "parallel"),
        name="diff_context",
    )(diff_lambda, subln, qk, qk, vt)
    return lat, ctx


def _f1_kernel(f_ref, u_ref, a_ref):
    l1 = u_ref.shape[0]
    r = _dot(f_ref[...], u_ref[...])
    a_ref[0] = r[:l1].astype(a_ref.dtype)
    a_ref[1] = r[l1:].astype(a_ref.dtype)


def _f2_kernel(g_ref, a_ref, y_ref):
    kb, n, bw = a_ref.shape[1:]
    for j in range(kb):
        r = _dot(g_ref[j, :, :n], a_ref[0, j]) + _dot(g_ref[j, :, n:], a_ref[1, j])
        y_ref[0, :, j * bw:(j + 1) * bw] = r[:n].astype(y_ref.dtype)
        y_ref[1, :, j * bw:(j + 1) * bw] = r[n:].astype(y_ref.dtype)


def _f3_kernel(yr_ref, yi_ref, c_ref, s_ref, o_ref):
    o_ref[...] = (_dot(yr_ref[...], c_ref[...]) + _dot(yi_ref[...], s_ref[...])).astype(o_ref.dtype)


def _fctx_kernel(f_ref, u_ref, c_ref, s_ref, o_ref):
    n = u_ref.shape[0]
    w = _dot(f_ref[...], u_ref[...])
    wr = w[:n].astype(BF16)
    wi = w[n:].astype(BF16)
    o_ref[...] = (_dot(wr, c_ref[...]) + _dot(wi, s_ref[...])).astype(o_ref.dtype)


def _angle(prod, n):
    return (2.0 * math.pi / n) * (prod % n).astype(F32)


def _dft_pair(n):
    k = jnp.arange(n, dtype=jnp.int32)
    ang = _angle(k[:, None] * k[None, :], n)
    return jnp.concatenate([jnp.cos(ang), -jnp.sin(ang)], axis=0)


def _fourier_tables(seq, ctx_len, gdim):
    l1 = seq // FFT_INNER
    f1 = _dft_pair(l1).astype(BF16)
    k1 = jnp.arange(l1, dtype=jnp.int32)[:, None, None]
    k2 = jnp.arange(FFT_INNER, dtype=jnp.int32)[None, :, None]
    t2 = jnp.arange(FFT_INNER, dtype=jnp.int32)[None, None, :]
    ang = _angle(t2 * (k1 + l1 * k2), seq)
    c, s = jnp.cos(ang), jnp.sin(ang)
    g2 = jnp.concatenate([jnp.concatenate([c, s], axis=2), jnp.concatenate([-s, c], axis=2)], axis=1).astype(BF16)
    kc = jnp.arange(gdim, dtype=jnp.int32)
    angc = _angle(kc[:, None] * kc[None, :], gdim)
    cc, sc = jnp.cos(angc), jnp.sin(angc)
    nl = 1.0 / math.sqrt(seq * gdim)
    nc = 1.0 / math.sqrt(ctx_len * gdim)
    return dict(f1=f1, g2=g2, c3=(cc * nl).astype(BF16), s3=(sc * nl).astype(BF16),
                fc=_dft_pair(ctx_len).astype(BF16), c3c=(cc * nc).astype(BF16), s3c=(sc * nc).astype(BF16))


def _fourier(pf, tabs, batch, seq, ctx_len):
    m, bw = pf.shape
    gdim = bw // FOURIER_GROUPS
    l1 = seq // FFT_INNER
    flat = FFT_INNER * bw
    cc1 = min(flat, FFT_STAGE1_COLS)
    a = pl.pallas_call(
        _f1_kernel,
        grid=(batch, flat // cc1),
        in_specs=[pl.BlockSpec((2 * l1, l1), lambda b, j: (0, 0)),
                  pl.BlockSpec((l1, cc1), lambda b, j: (b, j))],
        out_specs=pl.BlockSpec((None, 2, l1, cc1), lambda b, j: (b, 0, 0, j)),
        out_shape=jax.ShapeDtypeStruct((batch, 2, l1, flat), BF16),
        compiler_params=_params("parallel", "parallel"),
        name="fourier_stage1",
    )(tabs["f1"], pf.reshape(m // FFT_INNER, flat))
    kb = math.gcd(l1, FFT_K1_BLOCK)
    y = pl.pallas_call(
        _f2_kernel,
        grid=(batch, l1 // kb),
        in_specs=[pl.BlockSpec((kb, 2 * FFT_INNER, 2 * FFT_INNER), lambda b, k: (k, 0, 0)),
                  pl.BlockSpec((None, 2, kb, FFT_INNER, bw), lambda b, k: (b, 0, k, 0, 0))],
        out_specs=pl.BlockSpec((None, 2, FFT_INNER, kb * bw), lambda b, k: (b, 0, 0, k)),
        out_shape=jax.ShapeDtypeStruct((batch, 2, FFT_INNER, l1 * bw), BF16),
        compiler_params=_params("parallel", "parallel"),
        name="fourier_stage2",
    )(tabs["g2"], a.reshape(batch, 2, l1, FFT_INNER, bw))
    y4 = y.reshape(batch, 2, seq, bw)
    tm3 = min(seq, FFT_STAGE3_ROWS)
    nt = seq // tm3
    tab = pl.BlockSpec((gdim, gdim), lambda b, i, g: (0, 0))
    lat = pl.pallas_call(
        _f3_kernel,
        grid=(batch, nt, FOURIER_GROUPS),
        in_specs=[pl.BlockSpec((None, None, tm3, gdim), lambda b, i, g: (b, 0, i, g)),
                  pl.BlockSpec((None, None, tm3, gdim), lambda b, i, g: (b, 1, i, g)), tab, tab],
        out_specs=pl.BlockSpec((tm3, gdim), lambda b, i, g: (b * nt + i, g)),
        out_shape=jax.ShapeDtypeStruct((batch * seq, bw), BF16),
        compiler_params=_params("parallel", "parallel", "parallel"),
        name="fourier_stage3",
    )(y4, y4, tabs["c3"], tabs["s3"])
    ctx_row = batch * seq // ctx_len
    tab2 = pl.BlockSpec((gdim, gdim), lambda b, g: (0, 0))
    ctx = pl.pallas_call(
        _fctx_kernel,
        grid=(batch, FOURIER_GROUPS),
        in_specs=[pl.BlockSpec((2 * ctx_len, ctx_len), lambda b, g: (0, 0)),
                  pl.BlockSpec((ctx_len, gdim), lambda b, g: (ctx_row + b, g)), tab2, tab2],
        out_specs=pl.BlockSpec((ctx_len, gdim), lambda b, g: (b, g)),
        out_shape=jax.ShapeDtypeStruct((batch * ctx_len, bw), BF16),
        compiler_params=_params("parallel", "parallel"),
        name="fourier_context",
    )(tabs["fc"], pf, tabs["c3c"], tabs["s3c"])
    return lat, ctx


def _pick(n, prefs):
    for t in prefs:
        if n % t == 0:
            return t
    return n


def kernel(x, c, ctx, c_ctx, norm_g, ada_down, ada_up, ada_b, ffn_w13, ffn_w2, w_in, conv_w, swa_q_norm,
           swa_k_norm, swa_sink, diff_q_norm, diff_k_norm, diff_lambda, diff_subln, w_branch, gate_up, w_out):
    batch, seq, d = x.shape
    ctx_len = ctx.shape[1]
    depth = norm_g.shape[0]
    bw = d // 4
    rank = gate_up.shape[1]
    d_ff = ffn_w2.shape[2]
    n_lat = batch * seq
    n_ctx = batch * ctx_len
    m = n_lat + n_ctx
    tm, tr = ROW_TILE, ELT_ROW_TILE
    swa_hd = bw // SWA_HEADS
    diff_hd = bw // (2 * DIFF_HEADS)
    assert swa_hd == LANES and 2 * diff_hd == LANES
    assert seq % tm == 0 and n_ctx % tm == 0 and seq % FFT_INNER == 0 and seq % GRID_W == 0
    assert seq & (seq - 1) == 0 and ctx_len & (ctx_len - 1) == 0 and n_lat % ctx_len == 0
    assert w_in.shape[2] == 8 * bw + 2 * SWA_KV_HEADS * swa_hd + rank

    def mod_row_for(layer):
        def mod_row(i, tile):
            who = jnp.minimum(i // (seq // tile), batch)
            return (layer * (batch + 1) + who) * N_MOD
        return mod_row

    def rope_row(i, tile):
        return jnp.where(i < n_lat // tile, i % (seq // tile), seq // tile)

    cv = jnp.concatenate([c, c_ctx[None, :], jnp.zeros((SUBLANES - batch - 1, d), F32)], axis=0)
    mods = _ada(cv, ada_down, ada_up, ada_b)[:, :batch + 1]
    mods = mods.reshape(depth * (batch + 1) * N_MOD, 1, d)
    gains = norm_g.reshape(depth * 3, 1, d)

    w13 = ffn_w13.astype(BF16)
    w2 = ffn_w2.astype(BF16)
    w_branch_b = w_branch.astype(BF16)
    gate_up_b = gate_up.astype(BF16).reshape(depth, rank, -1)
    w_out_b = w_out.astype(BF16)
    kv_w = SWA_KV_HEADS * swa_hd
    sw_w = bw + 2 * kv_w
    cuts = (0, bw, 4 * bw, 4 * bw + sw_w, 7 * bw + sw_w, 7 * bw + sw_w + rank)
    w_f, w_cv, w_sw, w_df, w_z = (w_in[:, :, a:b].astype(BF16) for a, b in zip(cuts[:-1], cuts[1:]))

    cos_s, sin_s = _rope_tables(seq, swa_hd, tr)
    cos_d, sin_d = _rope_tables(seq, diff_hd, tr)
    tabs = _fourier_tables(seq, ctx_len, bw // FOURIER_GROUPS)

    tn_ff = _pick(d_ff, COL_TILES)
    tn_d = _pick(d, COL_TILES)
    tn_bw = _pick(bw, COL_TILES)
    tn_sw = _pick(sw_w, (SWA_COL_TILE,) + COL_TILES)
    swa_gains = jnp.concatenate([jnp.tile(swa_q_norm, (1, SWA_HEADS)), jnp.tile(swa_k_norm, (1, SWA_KV_HEADS)),
                                 jnp.ones((depth, kv_w), F32)], axis=1).reshape(depth, 1, sw_w)
    diff_gains = jnp.concatenate([jnp.tile(diff_q_norm, (1, 2 * DIFF_HEADS)),
                                  jnp.tile(diff_k_norm, (1, 2 * DIFF_HEADS))], axis=1).reshape(depth, 1, 2 * bw)
    subln = diff_subln.reshape(depth, 1, 2 * diff_hd)

    t = jnp.concatenate([x.reshape(n_lat, d), ctx.reshape(n_ctx, d)], axis=0)

    def free_tile(rows):
        return FREE_ROW_TILE if rows % FREE_ROW_TILE == 0 else tm

    def ffn_half(t, layer, sub, k0, rows):
        mod_row = mod_row_for(layer)
        h = _normmod(t, gains, layer * 3 + 2 * sub, mods, mod_row, k0, k0 + 1, tr)
        act = _mm_swiglu(h, w13, (layer, sub), tn_ff, tm)
        return _mm_resid(act, w2, (layer, sub), t, mods, mod_row, k0 + 2, 0.5, tn_d, tm, rows)

    for layer in range(depth):
        last = layer == depth - 1
        rows = n_lat if last else m
        mod_row = mod_row_for(layer)
        lam_init = 0.8 - 0.6 * math.exp(-0.3 * layer)
        t = ffn_half(t, layer, 0, 0, m)
        h = _normmod(t, gains, layer * 3 + 1, mods, mod_row, 3, 4, tr)
        tf = free_tile(m)
        pf = _mm(h, w_f, (layer,), tn_bw, BF16, tf)
        pcv = _mm(h, w_cv, (layer,), tn_bw, F32, tf)
        psw = _mm(h, w_sw, (layer,), tn_sw, F32, tf)
        pdf = _mm(h, w_df, (layer,), tn_bw, F32, tf)
        pz = _mm(h, w_z, (layer,), rank, BF16, tf)

        o_f = _fourier(pf, tabs, batch, seq, ctx_len) + (0,)
        o_c = _conv(pcv, conv_w, layer, seq, ctx_len, n_lat, tr)
        qkv_s = _prep(psw, sw_w, swa_gains[layer], cos_s, sin_s, swa_hd, swa_hd // 4,
                      SWA_HEADS + SWA_KV_HEADS, 0, 1.0, rope_row, tr)
        o_s = _swa(qkv_s, swa_sink[layer], swa_hd, batch, seq, ctx_len) + (0,)
        qk_d = _prep(pdf, 2 * bw, diff_gains[layer], cos_d, sin_d, diff_hd, diff_hd // 4,
                     2 * DIFF_HEADS, DIFF_HEADS, diff_hd ** -0.5 * LOG2_E, rope_row, tr)
        vt_d = _vt(pdf, 2, DIFF_HEADS, tr)
        o_d = _diff(qk_d, vt_d, diff_lambda, subln, layer, lam_init, batch, seq, ctx_len,
                    min(seq, DIFF_Q_TILE), min(seq, DIFF_K_TILE)) + (0,)

        merged = _merge([o_f, (o_c, o_c, n_lat // tm), o_s, o_d], pz, w_branch_b, gate_up_b, layer,
                        tn_d, tm, rows, n_lat)
        t = _mm_resid(merged, w_out_b, (layer,), t, mods, mod_row, 5, 1.0, tn_d, tm, rows)
        t = ffn_half(t, layer, 1, 6, rows)
    return t.reshape(batch, seq, d)
```

```python
import functools
import math

import jax
import jax.numpy as jnp
from jax import lax
from jax.experimental import pallas as pl
from jax.experimental.pallas import tpu as pltpu

F32 = jnp.float32
BF16 = jnp.bfloat16

EPS = 1e-6
NEG_INF = -1e30
ROPE_BASE = 10000.0
GRID_W = 64
N_MOD = 9
FOURIER_GROUPS = 4
SWA_HEADS = 8
SWA_KV_HEADS = 2
SWA_BLOCK = 128
SWA_WINDOW = 128
DIFF_HEADS = 8
LANES = 128
SUBLANES = 8
FFT_INNER = 128
FFT_K1_BLOCK = 4
FFT_STAGE1_COLS = 16384
FFT_STAGE3_ROWS = 2048

VMEM_LIMIT_BYTES = 56 * 1024 * 1024
ROW_TILE = 512
FREE_ROW_TILE = 768
ELT_ROW_TILE = 512
DIFF_Q_TILE = 1024
DIFF_K_TILE = 1024
DIFF_SCORE_BUFFERS = 2
LOG2_E = 1.4426950408889634
COL_TILES = (1024, 512, 256, 128)
SWA_COL_TILE = 768
SWA_Q_BLOCKS = 8

_HIGHEST = lax.Precision.HIGHEST
_NT_DIMS = (((1,), (1,)), ((), ()))


def _params(*sem):
    return pltpu.CompilerParams(dimension_semantics=sem, vmem_limit_bytes=VMEM_LIMIT_BYTES)


def _dot(a, b):
    return jnp.dot(a, b, preferred_element_type=F32)


def _ada_kernel(cv_ref, down_ref, up_ref, b_ref, o_ref, hid_ref):
    @pl.when(pl.program_id(1) == 0)
    def _():
        c = cv_ref[...]
        s = c * jax.nn.sigmoid(c)
        hid_ref[...] = jnp.dot(s, down_ref[...], precision=_HIGHEST, preferred_element_type=F32)

    o_ref[...] = jnp.dot(hid_ref[...], up_ref[...], precision=_HIGHEST,
                         preferred_element_type=F32) + b_ref[...]


def _ada(cv, down, up, b):
    depth, d, r = down.shape
    n = up.shape[2]
    tn = n // N_MOD
    return pl.pallas_call(
        _ada_kernel,
        grid=(depth, n // tn),
        in_specs=[pl.BlockSpec((SUBLANES, d), lambda l, j: (0, 0)),
                  pl.BlockSpec((None, d, r), lambda l, j: (l, 0, 0)),
                  pl.BlockSpec((None, r, tn), lambda l, j: (l, 0, j)),
                  pl.BlockSpec((None, 1, tn), lambda l, j: (l, 0, j))],
        out_specs=pl.BlockSpec((None, SUBLANES, tn), lambda l, j: (l, 0, j)),
        out_shape=jax.ShapeDtypeStruct((depth, SUBLANES, n), F32),
        scratch_shapes=[pltpu.VMEM((SUBLANES, r), F32)],
        compiler_params=_params("parallel", "arbitrary"),
        name="ada_modulation",
    )(cv, down, up, b.reshape(depth, 1, n))


def _normmod_kernel(t_ref, g_ref, sh_ref, sc_ref, o_ref):
    x = t_ref[...]
    ms = jnp.mean(x * x, axis=-1, keepdims=True)
    y = x * lax.rsqrt(ms + EPS) * g_ref[...]
    o_ref[...] = (y * (1.0 + sc_ref[...]) + sh_ref[...]).astype(o_ref.dtype)


def _normmod(t, gains, g_row, mods, mod_row, k_shift, k_scale, tr):
    m, d = t.shape
    vec = lambda idx: pl.BlockSpec((None, 1, d), idx)
    return pl.pallas_call(
        _normmod_kernel,
        grid=(m // tr,),
        in_specs=[pl.BlockSpec((tr, d), lambda i: (i, 0)),
                  vec(lambda i: (g_row, 0, 0)),
                  vec(lambda i: (mod_row(i, tr) + k_shift, 0, 0)),
                  vec(lambda i: (mod_row(i, tr) + k_scale, 0, 0))],
        out_specs=pl.BlockSpec((tr, d), lambda i: (i, 0)),
        out_shape=jax.ShapeDtypeStruct((m, d), BF16),
        compiler_params=_params("parallel"),
        name="norm_modulate",
    )(t, gains, mods, mods)


def _mm_kernel(x_ref, w_ref, o_ref):
    o_ref[...] = _dot(x_ref[...], w_ref[...]).astype(o_ref.dtype)


def _w_spec(widx, k, tn, col_block):
    lead = (None,) * len(widx)
    return pl.BlockSpec(lead + (k, tn), lambda j, i: widx + (0, col_block + j))


def _mm(x, w, widx, tn, out_dtype, tm):
    m, k = x.shape
    ncols = w.shape[-1]
    return pl.pallas_call(
        _mm_kernel,
        grid=(ncols // tn, m // tm),
        in_specs=[pl.BlockSpec((tm, k), lambda j, i: (i, 0)),
                  _w_spec(widx, k, tn, 0)],
        out_specs=pl.BlockSpec((tm, tn), lambda j, i: (i, j)),
        out_shape=jax.ShapeDtypeStruct((m, ncols), out_dtype),
        compiler_params=_params("parallel", "parallel"),
        name="matmul",
    )(x, w)


def _mm_swiglu_kernel(x_ref, wa_ref, wu_ref, o_ref):
    x = x_ref[...]
    a = _dot(x, wa_ref[...])
    u = _dot(x, wu_ref[...])
    o_ref[...] = (a * jax.nn.sigmoid(a) * u).astype(o_ref.dtype)


def _mm_swiglu(x, w13, widx, tn, tm):
    m, k = x.shape
    f = w13.shape[-1] // 2
    return pl.pallas_call(
        _mm_swiglu_kernel,
        grid=(f // tn, m // tm),
        in_specs=[pl.BlockSpec((tm, k), lambda j, i: (i, 0)),
                  _w_spec(widx, k, tn, 0),
                  _w_spec(widx, k, tn, f // tn)],
        out_specs=pl.BlockSpec((tm, tn), lambda j, i: (i, j)),
        out_shape=jax.ShapeDtypeStruct((m, f), BF16),
        compiler_params=_params("parallel", "parallel"),
        name="matmul_swiglu",
    )(x, w13, w13)


def _mm_resid_kernel(x_ref, w_ref, t_ref, g_ref, o_ref, *, coef):
    y = _dot(x_ref[...], w_ref[...])
    o_ref[...] = t_ref[...] + (coef * g_ref[...]) * y


def _mm_resid(x, w, widx, t, mods, mod_row, k_gate, coef, tn, tm, m_rows):
    k = x.shape[1]
    d = t.shape[1]
    return pl.pallas_call(
        functools.partial(_mm_resid_kernel, coef=coef),
        grid=(d // tn, m_rows // tm),
        in_specs=[pl.BlockSpec((tm, k), lambda j, i: (i, 0)),
                  _w_spec(widx, k, tn, 0),
                  pl.BlockSpec((tm, tn), lambda j, i: (i, j)),
                  pl.BlockSpec((None, 1, tn), lambda j, i: (mod_row(i, tm) + k_gate, 0, j))],
        out_specs=pl.BlockSpec((tm, tn), lambda j, i: (i, j)),
        out_shape=jax.ShapeDtypeStruct((m_rows, d), F32),
        compiler_params=_params("parallel", "parallel"),
        name="matmul_gated_residual",
    )(x, w, t, mods)


def _merge_kernel(*refs, nbr, n_lat_tiles):
    lat_refs, ctx_refs = refs[:nbr], refs[nbr:2 * nbr]
    z_ref, wb_ref = refs[2 * nbr:2 * nbr + 2]
    g_refs = refs[2 * nbr + 2:3 * nbr + 2]
    out_ref = refs[-1]

    def compute(o_refs):
        z = z_ref[...]
        acc = None
        for b in range(nbr):
            term = jax.nn.sigmoid(_dot(z, g_refs[b][...])) * _dot(o_refs[b][...], wb_ref[b])
            acc = term if acc is None else acc + term
        out_ref[...] = acc.astype(out_ref.dtype)

    is_lat = pl.program_id(1) < n_lat_tiles
    pl.when(is_lat)(lambda: compute(lat_refs))
    pl.when(jnp.logical_not(is_lat))(lambda: compute(ctx_refs))


def _merge(branches, z, w_branch, gate_up2, layer, tn, tm, m_rows, n_lat):
    bw = branches[0][0].shape[1]
    r = z.shape[1]
    nbr, d = w_branch.shape[1], w_branch.shape[3]
    nl = n_lat // tm
    lat = pl.BlockSpec((tm, bw), lambda j, i: (jnp.minimum(i, nl - 1), 0))
    ctx = lambda first: pl.BlockSpec((tm, bw), lambda j, i: (jnp.maximum(i - nl, 0) + first, 0))
    gate = lambda b: pl.BlockSpec((None, r, tn), lambda j, i: (layer, 0, b * (d // tn) + j))
    return pl.pallas_call(
        functools.partial(_merge_kernel, nbr=nbr, n_lat_tiles=nl),
        grid=(d // tn, m_rows // tm),
        in_specs=[lat] * nbr + [ctx(first) for _, _, first in branches]
                 + [pl.BlockSpec((tm, r), lambda j, i: (i, 0)),
                    pl.BlockSpec((None, nbr, bw, tn), lambda j, i: (layer, 0, 0, j))]
                 + [gate(b) for b in range(nbr)],
        out_specs=pl.BlockSpec((tm, tn), lambda j, i: (i, j)),
        out_shape=jax.ShapeDtypeStruct((m_rows, d), BF16),
        compiler_params=_params("parallel", "parallel"),
        name="gated_merge",
    )(*[a for a, _, _ in branches], *[c for _, c, _ in branches], z, w_branch, *([gate_up2] * nbr))


def _prep_kernel(x_ref, g_ref, cos_ref, sin_ref, o_ref, *, seg, half, n_rope, n_scaled, qscale):
    cos = cos_ref[...]
    sin = sin_ref[...]
    ri = lax.broadcasted_iota(jnp.int32, (2 * LANES, LANES), 0) & (LANES - 1)
    ci = lax.broadcasted_iota(jnp.int32, (2 * LANES, LANES), 1)
    seg_ind = jnp.where((ri & -seg) == (ci & -seg), 1.0, 0.0).astype(BF16)
    for c in range(o_ref.shape[1] // LANES):
        sl = slice(c * LANES, (c + 1) * LANES)
        x = x_ref[:, sl]
        if c >= n_rope:
            o_ref[:, sl] = x.astype(o_ref.dtype)
            continue
        lane = lax.broadcasted_iota(jnp.int32, x.shape, 1)
        ss = x * x
        ss_hi = ss.astype(BF16)
        ss_lo = (ss - ss_hi.astype(F32)).astype(BF16)
        ms = _dot(jnp.concatenate([ss_hi, ss_lo], axis=1), seg_ind) * (1.0 / seg)
        xn = x * lax.rsqrt(ms + EPS) * g_ref[:, sl]
        lower = pltpu.roll(xn, half, 1)
        upper = pltpu.roll(xn, LANES - half, 1)
        partner = jnp.where((lane & half) == 0, upper, lower)
        y = xn * cos + partner * sin
        if c < n_scaled:
            y = y * qscale
        o_ref[:, sl] = y.astype(o_ref.dtype)


def _prep(p, n, gains, cos_t, sin_t, seg, half, n_rope, n_scaled, qscale, rope_row, tr):
    m = p.shape[0]
    return pl.pallas_call(
        functools.partial(_prep_kernel, seg=seg, half=half, n_rope=n_rope, n_scaled=n_scaled, qscale=qscale),
        grid=(m // tr,),
        in_specs=[pl.BlockSpec((tr, n), lambda i: (i, 0)),
                  pl.BlockSpec((1, n), lambda i: (0, 0)),
                  pl.BlockSpec((tr, LANES), lambda i: (rope_row(i, tr), 0)),
                  pl.BlockSpec((tr, LANES), lambda i: (rope_row(i, tr), 0))],
        out_specs=pl.BlockSpec((tr, n), lambda i: (i, 0)),
        out_shape=jax.ShapeDtypeStruct((m, n), BF16),
        compiler_params=_params("parallel"),
        name="qk_norm_rope",
    )(p, gains, cos_t, sin_t)


def _rope_tables(seq, head_dim, ident_rows):
    rows = seq // GRID_W
    row = jnp.repeat(jnp.arange(rows, dtype=jnp.int32), GRID_W)
    col = jnp.tile(jnp.arange(GRID_W, dtype=jnp.int32), rows)
    axis_dim = head_dim // 2
    inv_freq = ROPE_BASE ** (-jnp.arange(0, axis_dim, 2, dtype=F32) / axis_dim)
    ar = row.astype(F32)[:, None] * inv_freq
    ac = col.astype(F32)[:, None] * inv_freq
    cr, sr, cc, sc = jnp.cos(ar), jnp.sin(ar), jnp.cos(ac), jnp.sin(ac)
    cos_t = jnp.concatenate([cr, cr, cc, cc], axis=-1)
    sin_t = jnp.concatenate([-sr, sr, -sc, sc], axis=-1)
    reps = LANES // head_dim
    cos_t = jnp.tile(cos_t, (1, reps))
    sin_t = jnp.tile(sin_t, (1, reps))
    cos_t = jnp.concatenate([cos_t, jnp.ones((ident_rows, LANES), F32)], axis=0)
    sin_t = jnp.concatenate([sin_t, jnp.zeros((ident_rows, LANES), F32)], axis=0)
    return cos_t, sin_t


def _conv_kernel(gb_ref, gc_ref, v_ref, gcp_ref, vp_ref, gcn_ref, vn_ref, w_ref, o_ref, *, seq, ctx_len, n_lat):
    tr = gb_ref.shape[0]
    i = pl.program_id(0)
    u = gc_ref[...] * v_ref[...]
    u_before = (gcp_ref[...] * vp_ref[...])[SUBLANES - 1:SUBLANES, :]
    u_after = (gcn_ref[...] * vn_ref[...])[0:1, :]
    r = lax.broadcasted_iota(jnp.int32, (tr, 1), 0)
    gr = i * tr + r
    is_lat = gr < n_lat
    pos = jnp.where(is_lat, gr & (seq - 1), (gr - n_lat) & (ctx_len - 1))
    last = jnp.where(is_lat, seq - 1, ctx_len - 1)
    u_m = jnp.where(r == 0, u_before, pltpu.roll(u, 1, 0))
    u_m = jnp.where(pos == 0, 0.0, u_m)
    u_p = jnp.where(r == tr - 1, u_after, pltpu.roll(u, tr - 1, 0))
    u_p = jnp.where(pos == last, 0.0, u_p)
    y = u_m * w_ref[0:1, :] + u * w_ref[1:2, :] + u_p * w_ref[2:3, :]
    o_ref[...] = (gb_ref[...] * y).astype(o_ref.dtype)


def _conv(pcv, conv_w, layer, seq, ctx_len, n_lat, tr):
    m = pcv.shape[0]
    bw = pcv.shape[1] // 3
    halo = tr // SUBLANES
    n_halo = m // SUBLANES
    main = lambda c: pl.BlockSpec((tr, bw), lambda i: (i, c))
    before = lambda c: pl.BlockSpec((SUBLANES, bw), lambda i: (jnp.maximum(i * halo - 1, 0), c))
    after = lambda c: pl.BlockSpec((SUBLANES, bw), lambda i: (jnp.minimum((i + 1) * halo, n_halo - 1), c))
    return pl.pallas_call(
        functools.partial(_conv_kernel, seq=seq, ctx_len=ctx_len, n_lat=n_lat),
        grid=(m // tr,),
        in_specs=[main(0), main(1), main(2), before(1), before(2), after(1), after(2),
                  pl.BlockSpec((None, 3, bw), lambda i: (layer, 0, 0))],
        out_specs=pl.BlockSpec((tr, bw), lambda i: (i, 0)),
        out_shape=jax.ShapeDtypeStruct((m, bw), BF16),
        compiler_params=_params("parallel"),
        name="short_conv",
    )(pcv, pcv, pcv, pcv, pcv, pcv, pcv, conv_w)


def _sink_column(sink_ref, h, groups, blk):
    r = lax.broadcasted_iota(jnp.int32, (groups * blk, 1), 0)
    col = jnp.zeros((groups * blk, 1), F32)
    for g in range(groups):
        col = jnp.where((r >= g * blk) & (r < (g + 1) * blk), sink_ref[h * groups + g], col)
    return col


def _stack_heads(q, groups, hd):
    return jnp.concatenate([q[:, g * hd:(g + 1) * hd] for g in range(groups)], axis=0)


def _swa_lat_kernel(sink_ref, q_ref, kp_ref, kc_ref, kn_ref, vp_ref, vc_ref, vn_ref, kx_ref, vx_ref, o_ref,
                    *, seq, groups, hd, scale, blk):
    n0 = pl.program_id(1) * (q_ref.shape[0] // blk)
    h = pl.program_id(2)
    k_all = jnp.concatenate([kp_ref[...], kc_ref[...], kn_ref[...]], axis=0)
    v_all = jnp.concatenate([vp_ref[...], vc_ref[...], vn_ref[...]], axis=0)
    kx = kx_ref[...]
    vx = vx_ref[...]
    sink = _sink_column(sink_ref, h, groups, blk)
    ri = lax.broadcasted_iota(jnp.int32, (groups * blk, 3 * blk), 0)
    ci = lax.broadcasted_iota(jnp.int32, (groups * blk, 3 * blk), 1)
    in_window = jnp.abs(ci - blk - (ri & (blk - 1))) <= SWA_WINDOW
    for qb in range(q_ref.shape[0] // blk):
        rows = slice(qb * blk, (qb + 1) * blk)
        q = _stack_heads(q_ref[rows, :], groups, hd)
        kw = k_all[qb * blk:(qb + 3) * blk]
        vw = v_all[qb * blk:(qb + 3) * blk]
        s_loc = lax.dot_general(q, kw, _NT_DIMS, preferred_element_type=F32) * scale
        s_ctx = lax.dot_general(q, kx, _NT_DIMS, preferred_element_type=F32) * scale
        kpos = (n0 + qb - 1) * blk + ci
        valid = in_window & (kpos >= 0) & (kpos < seq)
        s_loc = jnp.where(valid, s_loc, NEG_INF)
        m = jnp.maximum(jnp.maximum(jnp.max(s_ctx, axis=-1, keepdims=True),
                                    jnp.max(s_loc, axis=-1, keepdims=True)), sink)
        e_ctx = jnp.exp(s_ctx - m)
        e_loc = jnp.exp(s_loc - m)
        den = (jnp.sum(e_ctx, axis=-1, keepdims=True) + jnp.sum(e_loc, axis=-1, keepdims=True)
               + jnp.exp(sink - m))
        o = (_dot(e_ctx.astype(BF16), vx) + _dot(e_loc.astype(BF16), vw)) / den
        for g in range(groups):
            o_ref[rows, g * hd:(g + 1) * hd] = o[g * blk:(g + 1) * blk].astype(o_ref.dtype)


def _swa_ctx_kernel(sink_ref, q_ref, kx_ref, vx_ref, o_ref, *, groups, hd, scale):
    h = pl.program_id(1)
    blk = q_ref.shape[0]
    q = _stack_heads(q_ref[...], groups, hd)
    s = lax.dot_general(q, kx_ref[...], _NT_DIMS, preferred_element_type=F32) * scale
    sink = _sink_column(sink_ref, h, groups, blk)
    m = jnp.maximum(jnp.max(s, axis=-1, keepdims=True), sink)
    e = jnp.exp(s - m)
    den = jnp.sum(e, axis=-1, keepdims=True) + jnp.exp(sink - m)
    o = _dot(e.astype(BF16), vx_ref[...]) / den
    for g in range(groups):
        o_ref[:, g * hd:(g + 1) * hd] = o[g * blk:(g + 1) * blk].astype(o_ref.dtype)


def _swa(qkv, sink, hd, batch, seq, ctx_len):
    bw = SWA_HEADS * hd
    groups = SWA_HEADS // SWA_KV_HEADS
    gw = groups * hd
    scale = hd ** -0.5
    blk = SWA_BLOCK
    nb = seq // blk
    kcol = bw // hd
    vcol = kcol + SWA_KV_HEADS
    ctx_row = batch * seq // ctx_len
    smem = pl.BlockSpec(memory_space=pltpu.SMEM)

    qb = min(SWA_Q_BLOCKS, nb)
    ng = nb // qb

    def halo(col, first):
        def idx(b, n, h):
            return (b * nb + jnp.clip(n * qb + first, 0, nb - 1), col + h)
        return pl.BlockSpec((blk, hd), idx)

    own = lambda col: pl.BlockSpec((qb * blk, hd), lambda b, n, h: (b * ng + n, col + h))
    ctx_kv = lambda col: pl.BlockSpec((ctx_len, hd), lambda b, n, h: (ctx_row + b, col + h))
    lat = pl.pallas_call(
        functools.partial(_swa_lat_kernel, seq=seq, groups=groups, hd=hd, scale=scale, blk=blk),
        grid=(batch, ng, SWA_KV_HEADS),
        in_specs=[smem, pl.BlockSpec((qb * blk, gw), lambda b, n, h: (b * ng + n, h)),
                  halo(kcol, -1), own(kcol), halo(kcol, qb), halo(vcol, -1), own(vcol), halo(vcol, qb),
                  ctx_kv(kcol), ctx_kv(vcol)],
        out_specs=pl.BlockSpec((qb * blk, gw), lambda b, n, h: (b * ng + n, h)),
        out_shape=jax.ShapeDtypeStruct((batch * seq, bw), BF16),
        compiler_params=_params("parallel", "parallel", "parallel"),
        name="swa_latent",
    )(sink, qkv, qkv, qkv, qkv, qkv, qkv, qkv, qkv, qkv)
    ctx_kv2 = lambda col: pl.BlockSpec((ctx_len, hd), lambda b, h: (ctx_row + b, col + h))
    ctx = pl.pallas_call(
        functools.partial(_swa_ctx_kernel, groups=groups, hd=hd, scale=scale),
        grid=(batch, SWA_KV_HEADS),
        in_specs=[smem, pl.BlockSpec((ctx_len, gw), lambda b, h: (ctx_row + b, h)),
                  ctx_kv2(kcol), ctx_kv2(vcol)],
        out_specs=pl.BlockSpec((ctx_len, gw), lambda b, h: (b, h)),
        out_shape=jax.ShapeDtypeStruct((batch * ctx_len, bw), BF16),
        compiler_params=_params("parallel", "parallel"),
        name="swa_context",
    )(sink, qkv, qkv, qkv)
    return lat, ctx


def _vt_kernel(v_ref, o_ref):
    heads, vrows, tr = o_ref.shape
    hd2 = v_ref.shape[1] // heads
    for h in range(heads):
        o_ref[h, :hd2, :] = v_ref[:, h * hd2:(h + 1) * hd2].T.astype(o_ref.dtype)
        o_ref[h, hd2:, :] = jnp.ones((vrows - hd2, tr), o_ref.dtype)


def _vt(p, col_block, heads, tr):
    m = p.shape[0]
    hd2 = LANES
    vrows = hd2 + 2 * SUBLANES
    return pl.pallas_call(
        _vt_kernel,
        grid=(m // tr,),
        in_specs=[pl.BlockSpec((tr, heads * hd2), lambda i: (i, col_block))],
        out_specs=pl.BlockSpec((heads, vrows, tr), lambda i: (0, 0, i)),
        out_shape=jax.ShapeDtypeStruct((heads, vrows, m), BF16),
        compiler_params=_params("parallel"),
        name="v_transpose",
    )(p)


def _diff_kernel(dl_ref, g_ref, q_ref, kx_ref, vx_ref, *rest, lam_init, tk, with_latent):
    if with_latent:
        k_ref, v_ref, o_ref, m_ref, acc_ref, *bufs = rest
    else:
        o_ref, m_ref, acc_ref = rest
    tq, hd2 = q_ref.shape
    hd = hd2 // 2
    qt = q_ref[...].astype(F32).T
    row = lax.broadcasted_iota(jnp.int32, qt.shape, 0)
    qcat = jnp.concatenate([jnp.where(row < hd, qt, 0.0), jnp.where(row < hd, 0.0, qt)], axis=1).astype(BF16)

    def scores(k):
        return _dot(k, qcat)

    def absorb(s, vt):
        m_old = m_ref[...]
        m_new = jnp.maximum(m_old, jnp.max(s, axis=0, keepdims=True))
        alpha = jnp.exp2(m_old - m_new)
        p = jnp.exp2(s - m_new).astype(BF16)
        acc_ref[...] = alpha * acc_ref[...] + _dot(vt, p)
        m_ref[...] = m_new

    m_ref[...] = jnp.full(m_ref.shape, NEG_INF, F32)
    acc_ref[...] = jnp.zeros(acc_ref.shape, F32)
    if with_latent:
        n_chunks = k_ref.shape[0] // tk
        ahead = len(bufs) - 1

        def issue_scores(c):
            bufs[c % len(bufs)][...] = scores(k_ref[c * tk:(c + 1) * tk, :])

        for c in range(min(ahead, n_chunks)):
            issue_scores(c)
        absorb(scores(kx_ref[...]), vx_ref[...])
        for c in range(n_chunks):
            if c + ahead < n_chunks:
                issue_scores(c + ahead)
            absorb(bufs[c % len(bufs)][...], v_ref[:, c * tk:(c + 1) * tk])
    else:
        absorb(scores(kx_ref[...]), vx_ref[...])
    acc = acc_ref[...]
    on = acc[:hd2] / acc[hd2:hd2 + 1]
    dl = dl_ref[...]
    lam = (jnp.exp(jnp.sum(dl[0:1] * dl[1:2], axis=-1, keepdims=True))
           - jnp.exp(jnp.sum(dl[2:3] * dl[3:4], axis=-1, keepdims=True)) + lam_init)
    o = (on[:, :tq] - lam * on[:, tq:]).T
    y = o * lax.rsqrt(jnp.mean(o * o, axis=-1, keepdims=True) + EPS) * g_ref[...]
    o_ref[...] = (y * (1.0 - lam_init)).astype(o_ref.dtype)


def _diff(qk, vt, diff_lambda, subln, layer, lam_init, batch, seq, ctx_len, tq, tk):
    bw = qk.shape[1] // 2
    hd2 = bw // DIFF_HEADS
    vrows = vt.shape[1]
    flags = dict(lam_init=lam_init, tk=tk)
    kcol = DIFF_HEADS
    ctx_row = batch * seq // ctx_len
    nq = seq // tq
    dl_spec3 = pl.BlockSpec((None,) + diff_lambda.shape[1:], lambda b, h, i: (layer, 0, 0))
    g_spec3 = pl.BlockSpec((None, 1, hd2), lambda b, h, i: (layer, 0, 0))
    lat = pl.pallas_call(
        functools.partial(_diff_kernel, with_latent=True, **flags),
        grid=(batch, DIFF_HEADS, nq),
        in_specs=[dl_spec3, g_spec3,
                  pl.BlockSpec((tq, hd2), lambda b, h, i: (b * nq + i, h)),
                  pl.BlockSpec((ctx_len, hd2), lambda b, h, i: (ctx_row + b, kcol + h)),
                  pl.BlockSpec((None, vrows, ctx_len), lambda b, h, i: (h, 0, ctx_row + b)),
                  pl.BlockSpec((seq, hd2), lambda b, h, i: (b, kcol + h)),
                  pl.BlockSpec((None, vrows, seq), lambda b, h, i: (h, 0, b))],
        out_specs=pl.BlockSpec((tq, hd2), lambda b, h, i: (b * nq + i, h)),
        out_shape=jax.ShapeDtypeStruct((batch * seq, bw), BF16),
        scratch_shapes=[pltpu.VMEM((1, 2 * tq), F32), pltpu.VMEM((vrows, 2 * tq), F32),
                        ] + [pltpu.VMEM((tk, 2 * tq), F32)] * DIFF_SCORE_BUFFERS,
        compiler_params=_params("parallel", "parallel", "arbitrary"),
        name="diff_latent",
    )(diff_lambda, subln, qk, qk, vt, qk, vt)
    dl_spec2 = pl.BlockSpec((None,) + diff_lambda.shape[1:], lambda b, h: (layer, 0, 0))
    g_spec2 = pl.BlockSpec((None, 1, hd2), lambda b, h: (layer, 0, 0))
    ctx = pl.pallas_call(
        functools.partial(_diff_kernel, with_latent=False, **flags),
        grid=(batch, DIFF_HEADS),
        in_specs=[dl_spec2, g_spec2,
                  pl.BlockSpec((ctx_len, hd2), lambda b, h: (ctx_row + b, h)),
                  pl.BlockSpec((ctx_len, hd2), lambda b, h: (ctx_row + b, kcol + h)),
                  pl.BlockSpec((None, vrows, ctx_len), lambda b, h: (h, 0, ctx_row + b))],
        out_specs=pl.BlockSpec((ctx_len, hd2), lambda b, h: (b, h)),
        out_shape=jax.ShapeDtypeStruct((batch * ctx_len, bw), BF16),
        scratch_shapes=[pltpu.VMEM((1, 2 * ctx_len), F32), pltpu.VMEM((vrows, 2 * ctx_len), F32)],
        compiler_params=_params("parallel", "parallel"),
        name="diff_context",
    )(diff_lambda, subln, qk, qk, vt)
    return lat, ctx


def _f1_kernel(f_ref, u_ref, a_ref):
    l1 = u_ref.shape[0]
    r = _dot(f_ref[...], u_ref[...])
    a_ref[0] = r[:l1].astype(a_ref.dtype)
    a_ref[1] = r[l1:].astype(a_ref.dtype)


def _f2_kernel(g_ref, a_ref, y_ref):
    kb, n, bw = a_ref.shape[1:]
    for j in range(kb):
        r = _dot(g_ref[j, :, :n], a_ref[0, j]) + _dot(g_ref[j, :, n:], a_ref[1, j])
        y_ref[0, :, j * bw:(j + 1) * bw] = r[:n].astype(y_ref.dtype)
        y_ref[1, :, j * bw:(j + 1) * bw] = r[n:].astype(y_ref.dtype)


def _f3_kernel(yr_ref, yi_ref, c_ref, s_ref, o_ref):
    o_ref[...] = (_dot(yr_ref[...], c_ref[...]) + _dot(yi_ref[...], s_ref[...])).astype(o_ref.dtype)


def _fctx_kernel(f_ref, u_ref, c_ref, s_ref, o_ref):
    n = u_ref.shape[0]
    w = _dot(f_ref[...], u_ref[...])
    wr = w[:n].astype(BF16)
    wi = w[n:].astype(BF16)
    o_ref[...] = (_dot(wr, c_ref[...]) + _dot(wi, s_ref[...])).astype(o_ref.dtype)


def _angle(prod, n):
    return (2.0 * math.pi / n) * (prod % n).astype(F32)


def _dft_pair(n):
    k = jnp.arange(n, dtype=jnp.int32)
    ang = _angle(k[:, None] * k[None, :], n)
    return jnp.concatenate([jnp.cos(ang), -jnp.sin(ang)], axis=0)


def _fourier_tables(seq, ctx_len, gdim):
    l1 = seq // FFT_INNER
    f1 = _dft_pair(l1).astype(BF16)
    k1 = jnp.arange(l1, dtype=jnp.int32)[:, None, None]
    k2 = jnp.arange(FFT_INNER, dtype=jnp.int32)[None, :, None]
    t2 = jnp.arange(FFT_INNER, dtype=jnp.int32)[None, None, :]
    ang = _angle(t2 * (k1 + l1 * k2), seq)
    c, s = jnp.cos(ang), jnp.sin(ang)
    g2 = jnp.concatenate([jnp.concatenate([c, s], axis=2), jnp.concatenate([-s, c], axis=2)], axis=1).astype(BF16)
    kc = jnp.arange(gdim, dtype=jnp.int32)
    angc = _angle(kc[:, None] * kc[None, :], gdim)
    cc, sc = jnp.cos(angc), jnp.sin(angc)
    nl = 1.0 / math.sqrt(seq * gdim)
    nc = 1.0 / math.sqrt(ctx_len * gdim)
    return dict(f1=f1, g2=g2, c3=(cc * nl).astype(BF16), s3=(sc * nl).astype(BF16),
                fc=_dft_pair(ctx_len).astype(BF16), c3c=(cc * nc).astype(BF16), s3c=(sc * nc).astype(BF16))


def _fourier(pf, tabs, batch, seq, ctx_len):
    m, bw = pf.shape
    gdim = bw // FOURIER_GROUPS
    l1 = seq // FFT_INNER
    flat = FFT_INNER * bw
    cc1 = min(flat, FFT_STAGE1_COLS)
    a = pl.pallas_call(
        _f1_kernel,
        grid=(batch, flat // cc1),
        in_specs=[pl.BlockSpec((2 * l1, l1), lambda b, j: (0, 0)),
                  pl.BlockSpec((l1, cc1), lambda b, j: (b, j))],
        out_specs=pl.BlockSpec((None, 2, l1, cc1), lambda b, j: (b, 0, 0, j)),
        out_shape=jax.ShapeDtypeStruct((batch, 2, l1, flat), BF16),
        compiler_params=_params("parallel", "parallel"),
        name="fourier_stage1",
    )(tabs["f1"], pf.reshape(m // FFT_INNER, flat))
    kb = math.gcd(l1, FFT_K1_BLOCK)
    y = pl.pallas_call(
        _f2_kernel,
        grid=(batch, l1 // kb),
        in_specs=[pl.BlockSpec((kb, 2 * FFT_INNER, 2 * FFT_INNER), lambda b, k: (k, 0, 0)),
                  pl.BlockSpec((None, 2, kb, FFT_INNER, bw), lambda b, k: (b, 0, k, 0, 0))],
        out_specs=pl.BlockSpec((None, 2, FFT_INNER, kb * bw), lambda b, k: (b, 0, 0, k)),
        out_shape=jax.ShapeDtypeStruct((batch, 2, FFT_INNER, l1 * bw), BF16),
        compiler_params=_params("parallel", "parallel"),
        name="fourier_stage2",
    )(tabs["g2"], a.reshape(batch, 2, l1, FFT_INNER, bw))
    y4 = y.reshape(batch, 2, seq, bw)
    tm3 = min(seq, FFT_STAGE3_ROWS)
    nt = seq // tm3
    tab = pl.BlockSpec((gdim, gdim), lambda b, i, g: (0, 0))
    lat = pl.pallas_call(
        _f3_kernel,
        grid=(batch, nt, FOURIER_GROUPS),
        in_specs=[pl.BlockSpec((None, None, tm3, gdim), lambda b, i, g: (b, 0, i, g)),
                  pl.BlockSpec((None, None, tm3, gdim), lambda b, i, g: (b, 1, i, g)), tab, tab],
        out_specs=pl.BlockSpec((tm3, gdim), lambda b, i, g: (b * nt + i, g)),
        out_shape=jax.ShapeDtypeStruct((batch * seq, bw), BF16),
        compiler_params=_params("parallel", "parallel", "parallel"),
        name="fourier_stage3",
    )(y4, y4, tabs["c3"], tabs["s3"])
    ctx_row = batch * seq // ctx_len
    tab2 = pl.BlockSpec((gdim, gdim), lambda b, g: (0, 0))
    ctx = pl.pallas_call(
        _fctx_kernel,
        grid=(batch, FOURIER_GROUPS),
        in_specs=[pl.BlockSpec((2 * ctx_len, ctx_len), lambda b, g: (0, 0)),
                  pl.BlockSpec((ctx_len, gdim), lambda b, g: (ctx_row + b, g)), tab2, tab2],
        out_specs=pl.BlockSpec((ctx_len, gdim), lambda b, g: (b, g)),
        out_shape=jax.ShapeDtypeStruct((batch * ctx_len, bw), BF16),
        compiler_params=_params("parallel", "parallel"),
        name="fourier_context",
    )(tabs["fc"], pf, tabs["c3c"], tabs["s3c"])
    return lat, ctx


def _pick(n, prefs):
    for t in prefs:
        if n % t == 0:
            return t
    return n


def kernel(x, c, ctx, c_ctx, norm_g, ada_down, ada_up, ada_b, ffn_w13, ffn_w2, w_in, conv_w, swa_q_norm,
           swa_k_norm, swa_sink, diff_q_norm, diff_k_norm, diff_lambda, diff_subln, w_branch, gate_up, w_out):
    batch, seq, d = x.shape
    ctx_len = ctx.shape[1]
    depth = norm_g.shape[0]
    bw = d // 4
    rank = gate_up.shape[1]
    d_ff = ffn_w2.shape[2]
    n_lat = batch * seq
    n_ctx = batch * ctx_len
    m = n_lat + n_ctx
    tm, tr = ROW_TILE, ELT_ROW_TILE
    swa_hd = bw // SWA_HEADS
    diff_hd = bw // (2 * DIFF_HEADS)
    assert swa_hd == LANES and 2 * diff_hd == LANES
    assert seq % tm == 0 and n_ctx % tm == 0 and seq % FFT_INNER == 0 and seq % GRID_W == 0
    assert seq & (seq - 1) == 0 and ctx_len & (ctx_len - 1) == 0 and n_lat % ctx_len == 0
    assert w_in.shape[2] == 8 * bw + 2 * SWA_KV_HEADS * swa_hd + rank

    def mod_row_for(layer):
        def mod_row(i, tile):
            who = jnp.minimum(i // (seq // tile), batch)
            return (layer * (batch + 1) + who) * N_MOD
        return mod_row

    def rope_row(i, tile):
        return jnp.where(i < n_lat // tile, i % (seq // tile), seq // tile)

    cv = jnp.concatenate([c, c_ctx[None, :], jnp.zeros((SUBLANES - batch - 1, d), F32)], axis=0)
    mods = _ada(cv, ada_down, ada_up, ada_b)[:, :batch + 1]
    mods = mods.reshape(depth * (batch + 1) * N_MOD, 1, d)
    gains = norm_g.reshape(depth * 3, 1, d)

    w13 = ffn_w13.astype(BF16)
    w2 = ffn_w2.astype(BF16)
    w_branch_b = w_branch.astype(BF16)
    gate_up_b = gate_up.astype(BF16).reshape(depth, rank, -1)
    w_out_b = w_out.astype(BF16)
    kv_w = SWA_KV_HEADS * swa_hd
    sw_w = bw + 2 * kv_w
    cuts = (0, bw, 4 * bw, 4 * bw + sw_w, 7 * bw + sw_w, 7 * bw + sw_w + rank)
    w_f, w_cv, w_sw, w_df, w_z = (w_in[:, :, a:b].astype(BF16) for a, b in zip(cuts[:-1], cuts[1:]))

    cos_s, sin_s = _rope_tables(seq, swa_hd, tr)
    cos_d, sin_d = _rope_tables(seq, diff_hd, tr)
    tabs = _fourier_tables(seq, ctx_len, bw // FOURIER_GROUPS)

    tn_ff = _pick(d_ff, COL_TILES)
    tn_d = _pick(d, COL_TILES)
    tn_bw = _pick(bw, COL_TILES)
    tn_sw = _pick(sw_w, (SWA_COL_TILE,) + COL_TILES)
    swa_gains = jnp.concatenate([jnp.tile(swa_q_norm, (1, SWA_HEADS)), jnp.tile(swa_k_norm, (1, SWA_KV_HEADS)),
                                 jnp.ones((depth, kv_w), F32)], axis=1).reshape(depth, 1, sw_w)
    diff_gains = jnp.concatenate([jnp.tile(diff_q_norm, (1, 2 * DIFF_HEADS)),
                                  jnp.tile(diff_k_norm, (1, 2 * DIFF_HEADS))], axis=1).reshape(depth, 1, 2 * bw)
    subln = diff_subln.reshape(depth, 1, 2 * diff_hd)

    t = jnp.concatenate([x.reshape(n_lat, d), ctx.reshape(n_ctx, d)], axis=0)

    def free_tile(rows):
        return FREE_ROW_TILE if rows % FREE_ROW_TILE == 0 else tm

    def ffn_half(t, layer, sub, k0, rows):
        mod_row = mod_row_for(layer)
        h = _normmod(t, gains, layer * 3 + 2 * sub, mods, mod_row, k0, k0 + 1, tr)
        act = _mm_swiglu(h, w13, (layer, sub), tn_ff, tm)
        return _mm_resid(act, w2, (layer, sub), t, mods, mod_row, k0 + 2, 0.5, tn_d, tm, rows)

    for layer in range(depth):
        last = layer == depth - 1
        rows = n_lat if last else m
        mod_row = mod_row_for(layer)
        lam_init = 0.8 - 0.6 * math.exp(-0.3 * layer)
        t = ffn_half(t, layer, 0, 0, m)
        h = _normmod(t, gains, layer * 3 + 1, mods, mod_row, 3, 4, tr)
        tf = free_tile(m)
        pf = _mm(h, w_f, (layer,), tn_bw, BF16, tf)
        pcv = _mm(h, w_cv, (layer,), tn_bw, F32, tf)
        psw = _mm(h, w_sw, (layer,), tn_sw, F32, tf)
        pdf = _mm(h, w_df, (layer,), tn_bw, F32, tf)
        pz = _mm(h, w_z, (layer,), rank, BF16, tf)

        o_f = _fourier(pf, tabs, batch, seq, ctx_len) + (0,)
        o_c = _conv(pcv, conv_w, layer, seq, ctx_len, n_lat, tr)
        qkv_s = _prep(psw, sw_w, swa_gains[layer], cos_s, sin_s, swa_hd, swa_hd // 4,
                      SWA_HEADS + SWA_KV_HEADS, 0, 1.0, rope_row, tr)
        o_s = _swa(qkv_s, swa_sink[layer], swa_hd, batch, seq, ctx_len) + (0,)
        qk_d = _prep(pdf, 2 * bw, diff_gains[layer], cos_d, sin_d, diff_hd, diff_hd // 4,
                     2 * DIFF_HEADS, DIFF_HEADS, diff_hd ** -0.5 * LOG2_E, rope_row, tr)
        vt_d = _vt(pdf, 2, DIFF_HEADS, tr)
        o_d = _diff(qk_d, vt_d, diff_lambda, subln, layer, lam_init, batch, seq, ctx_len,
                    min(seq, DIFF_Q_TILE), min(seq, DIFF_K_TILE)) + (0,)

        merged = _merge([o_f, (o_c, o_c, n_lat // tm), o_s, o_d], pz, w_branch_b, gate_up_b, layer,
                        tn_d, tm, rows, n_lat)
        t = _mm_resid(merged, w_out_b, (layer,), t, mods, mod_row, 5, 1.0, tn_d, tm, rows)
        t = ffn_half(t, layer, 1, 6, rows)
    return t.reshape(batch, seq, d)
```
